```python
import jax, jax.numpy as jnp
from jax import lax
import numpy as np

D_MODEL = 1024
BATCH = 8
SEQ = 4096
DEPTH = 2

CHUNK = 64
D_MIX = D_MODEL

GLA_DV = 96
GLA_DK = 48
GLA_WIDTH = 3 * D_MODEL // 8
GLA_HEADS = GLA_WIDTH // GLA_DV
GLA_KEY_WIDTH = GLA_HEADS * GLA_DK
GLA_GATE_RANK = 16
GLA_GATE_TAU = 16.0

CONV_WIDTH = D_MODEL // 4
CONV_KERNEL = 31

ATT_HEAD_DIM = 64
ATT_WIDTH = 3 * D_MODEL // 8
ATT_HEADS = ATT_WIDTH // ATT_HEAD_DIM
ATT_LEFT_CHUNKS = 8
ATT_BAND_CHUNKS = ATT_LEFT_CHUNKS + 1
ATT_BAND = ATT_BAND_CHUNKS * CHUNK
MAX_REL_DIST = 128
N_REL = 2 * MAX_REL_DIST + 1

D_FF = 4 * D_MODEL

EPS = 1e-6
NEG_INF = -1e30

IN_SIZES = (
    GLA_KEY_WIDTH,
    GLA_KEY_WIDTH,
    GLA_WIDTH,
    GLA_WIDTH,
    GLA_GATE_RANK,
    2 * CONV_WIDTH,
    ATT_WIDTH,
    ATT_WIDTH,
    ATT_WIDTH,
)
D_IN = int(sum(IN_SIZES))
IN_SPLITS = [int(s) for s in np.cumsum(IN_SIZES)[:-1]]

kernel_name = "hybrid_gla_conformer_chunkattn_encoder"


def rmsnorm(x, g):
    xf = x.astype(jnp.float32)
    y = xf * lax.rsqrt(jnp.mean(xf * xf, axis=-1, keepdims=True) + EPS)
    return (y * g.astype(jnp.float32)).astype(x.dtype)


def gla_mixer(q, k, v, g, gate_lr, w_gate, b_gate, out_norm):
    dtype = v.dtype
    B, S = q.shape[:2]
    nc = S // CHUNK
    f32 = jnp.float32
    shp_k = (B, nc, CHUNK, GLA_HEADS, GLA_DK)
    qf = q.astype(f32).reshape(shp_k) * (GLA_DK ** -0.5)
    kf = k.astype(f32).reshape(shp_k)
    vf = v.astype(f32).reshape(B, nc, CHUNK, GLA_HEADS, GLA_DV)
    z = gate_lr.astype(f32) @ w_gate.astype(f32) + b_gate.astype(f32)
    log_a = (jax.nn.log_sigmoid(z) / GLA_GATE_TAU).reshape(shp_k)
    cum = jnp.cumsum(log_a, axis=2)
    end = cum[:, :, -1:]
    k_dec = kf * jnp.exp(end - cum)
    chunk_decay = jnp.exp(end[:, :, 0])
    kv = jnp.einsum('bnchk,bnchv->bnhkv', k_dec, vf)

    def step(state, inp):
        a, kv_c = inp
        state = a[..., None] * state + kv_c
        return state, state

    init = jnp.zeros((B, GLA_HEADS, GLA_DK, GLA_DV), f32)
    _, states = lax.scan(step, init, (jnp.moveaxis(chunk_decay, 1, 0), jnp.moveaxis(kv, 1, 0)))
    o = jnp.einsum('bnchk,nbhkv->bnchv', qf, states)
    o = o.reshape(B, S, GLA_HEADS, GLA_DV)
    o = o * lax.rsqrt(jnp.mean(o * o, axis=-1, keepdims=True) + EPS)
    o = o.reshape(B, S, GLA_WIDTH) * out_norm.astype(f32)
    o = o * jax.nn.silu(g.astype(f32))
    return o.astype(dtype)


def conv_mixer(u, w_dw, b_dw, ln_g, ln_b):
    dtype = u.dtype
    a, b = jnp.split(u, 2, axis=-1)
    h = a * jax.nn.sigmoid(b)
    h = lax.conv_general_dilated(
        h, w_dw.reshape(CONV_KERNEL, 1, CONV_WIDTH).astype(h.dtype),
        window_strides=(1,), padding=((CONV_KERNEL - 1, 0),),
        dimension_numbers=('NWC', 'WIO', 'NWC'), feature_group_count=CONV_WIDTH)
    hf = h.astype(jnp.float32) + b_dw.astype(jnp.float32)
    mu = jnp.mean(hf, axis=-1, keepdims=True)
    var = jnp.mean(jnp.square(hf - mu), axis=-1, keepdims=True)
    hf = (hf - mu) * lax.rsqrt(var + EPS) * ln_g.astype(jnp.float32) + ln_b.astype(jnp.float32)
    return jax.nn.silu(hf).astype(dtype)


def chunk_attention(q, k, v, rel_bias):
    dtype = v.dtype
    B, S = q.shape[:2]
    nc = S // CHUNK
    shp = (B, nc, CHUNK, ATT_HEADS, ATT_HEAD_DIM)
    qc, kc, vc = q.reshape(shp), k.reshape(shp), v.reshape(shp)
    pad = ((0, 0), (ATT_LEFT_CHUNKS, 0), (0, 0), (0, 0), (0, 0))
    kp, vp = jnp.pad(kc, pad), jnp.pad(vc, pad)
    k_band = jnp.concatenate([kp[:, w:w + nc] for w in range(ATT_BAND_CHUNKS)], axis=2)
    v_band = jnp.concatenate([vp[:, w:w + nc] for w in range(ATT_BAND_CHUNKS)], axis=2)
    scores = jnp.einsum('bnqhd,bnkhd->bnhqk', qc, k_band).astype(jnp.float32) * (ATT_HEAD_DIM ** -0.5)
    q_pos = np.arange(CHUNK)[:, None]
    k_pos = np.arange(ATT_BAND)[None, :] - ATT_LEFT_CHUNKS * CHUNK
    rel_idx = np.clip(q_pos - k_pos, -MAX_REL_DIST, MAX_REL_DIST) + MAX_REL_DIST
    bias = rel_bias.astype(jnp.float32)[:, rel_idx]
    key_chunk = np.arange(nc)[:, None] - ATT_LEFT_CHUNKS + np.repeat(np.arange(ATT_BAND_CHUNKS), CHUNK)[None, :]
    valid = key_chunk >= 0
    scores = jnp.where(valid[None, :, None, None, :], scores + bias[None, None], NEG_INF)
    p = jax.nn.softmax(scores, axis=-1).astype(dtype)
    o = jnp.einsum('bnhqk,bnkhd->bnqhd', p, v_band)
    return o.reshape(B, S, ATT_WIDTH)


def sq_relu_mlp(x, w_up, w_down):
    h = jax.nn.relu(x @ w_up)
    return (h * h) @ w_down


def setup_inputs(seed: int = 0) -> dict:
    key = jax.random.key(seed)
    ks = jax.random.split(key, 20)
    f32 = jnp.float32
    nrm = lambda k, shape, s: (jax.random.normal(k, shape, f32) * s)
    return {
        "x": nrm(ks[0], (BATCH, SEQ, D_MODEL), 1.0),
        "norm_mix": 1.0 + nrm(ks[1], (DEPTH, D_MODEL), 0.01),
        "w_in": nrm(ks[2], (DEPTH, D_MODEL, D_IN), D_MODEL ** -0.5),
        "w_gla_gate": nrm(ks[3], (DEPTH, GLA_GATE_RANK, GLA_KEY_WIDTH), GLA_GATE_RANK ** -0.5),
        "b_gla_gate": nrm(ks[4], (DEPTH, GLA_KEY_WIDTH), 0.1),
        "gla_norm": 1.0 + nrm(ks[5], (DEPTH, GLA_WIDTH), 0.01),
        "w_dw": nrm(ks[6], (DEPTH, CONV_KERNEL, CONV_WIDTH), CONV_KERNEL ** -0.5),
        "b_dw": nrm(ks[7], (DEPTH, CONV_WIDTH), 0.01),
        "conv_ln_g": 1.0 + nrm(ks[8], (DEPTH, CONV_WIDTH), 0.01),
        "conv_ln_b": nrm(ks[9], (DEPTH, CONV_WIDTH), 0.01),
        "rel_bias": nrm(ks[10], (DEPTH, ATT_HEADS, N_REL), 0.1),
        "w_out": nrm(ks[11], (DEPTH, D_MIX, D_MODEL), D_MIX ** -0.5),
        "norm_ffn": 1.0 + nrm(ks[12], (DEPTH, D_MODEL), 0.01),
        "w_up": nrm(ks[13], (DEPTH, D_MODEL, D_FF), D_MODEL ** -0.5),
        "w_down": nrm(ks[14], (DEPTH, D_FF, D_MODEL), D_FF ** -0.5),
        "norm_final": 1.0 + nrm(ks[15], (D_MODEL,), 0.01),
    }


def reference(x, norm_mix, w_in, w_gla_gate, b_gla_gate, gla_norm, w_dw, b_dw,
              conv_ln_g, conv_ln_b, rel_bias, w_out, norm_ffn, w_up, w_down, norm_final):
    h = x
    for l in range(DEPTH):
        xn = rmsnorm(h, norm_mix[l])
        proj = xn @ w_in[l]
        (g_q, g_k, g_v, g_g, g_lr, c_u, a_q, a_k, a_v) = jnp.split(proj, IN_SPLITS, axis=-1)
        o_gla = gla_mixer(g_q, g_k, g_v, g_g, g_lr, w_gla_gate[l], b_gla_gate[l], gla_norm[l])
        o_conv = conv_mixer(c_u, w_dw[l], b_dw[l], conv_ln_g[l], conv_ln_b[l])
        o_att = chunk_attention(a_q, a_k, a_v, rel_bias[l])
        mixed = jnp.concatenate([o_gla, o_conv, o_att], axis=-1)
        h = h + mixed @ w_out[l]
        h = h + sq_relu_mlp(rmsnorm(h, norm_ffn[l]), w_up[l], w_down[l])
    return rmsnorm(h, norm_final)
```

```python
import functools

import numpy as np
import jax
import jax.numpy as jnp
from jax import lax
from jax.experimental import pallas as pl
from jax.experimental.pallas import tpu as pltpu

D_MODEL = 1024
DEPTH = 2
CHUNK = 64
GLA_DV, GLA_DK, GLA_HEADS = 96, 48, 4
GLA_WIDTH = GLA_HEADS * GLA_DV
GLA_KEY_WIDTH = GLA_HEADS * GLA_DK
GLA_GATE_RANK = 16
GLA_GATE_TAU = 16.0
CONV_WIDTH = 256
CONV_KERNEL = 31
ATT_HEAD_DIM = 64
ATT_HEADS = 6
ATT_WIDTH = ATT_HEADS * ATT_HEAD_DIM
ATT_LEFT_CHUNKS = 8
MAX_REL_DIST = 128
D_FF = 4 * D_MODEL
EPS = 1e-6
NEG_INF = -1e30

LANES = 128
VMEM_LIMIT_BYTES = 56 * 1024 * 1024

TS = 512
NCH = TS // CHUNK
HIST = ATT_LEFT_CHUNKS * CHUNK
QB = 256
KW = QB + HIST
CONV_PAD = 32
TM = 512
FF_CHUNK = 1024

KEY_PAD = 256
LR_PAD = LANES
_SEGS = (("gq", GLA_KEY_WIDTH, KEY_PAD), ("gk", GLA_KEY_WIDTH, KEY_PAD), ("gv", GLA_WIDTH, GLA_WIDTH),
         ("gg", GLA_WIDTH, GLA_WIDTH), ("lr", GLA_GATE_RANK, LR_PAD), ("ca", CONV_WIDTH, CONV_WIDTH),
         ("cb", CONV_WIDTH, CONV_WIDTH), ("aq", ATT_WIDTH, ATT_WIDTH), ("ak", ATT_WIDTH, ATT_WIDTH),
         ("av", ATT_WIDTH, ATT_WIDTH))
SEG = {}
_off = 0
for _name, _w, _wp in _SEGS:
    SEG[_name] = (_off, _off + _wp)
    _off += _wp
D_IN_PAD = _off

F32 = jnp.float32
BF16 = jnp.bfloat16


def _dot(a, b):
    return jnp.dot(a, b, preferred_element_type=F32)


def _dot_nt(a, b):
    return lax.dot_general(a, b, (((1,), (1,)), ((), ())), preferred_element_type=F32)


def _dot_tn(a, b):
    return lax.dot_general(a, b, (((0,), (0,)), ((), ())), preferred_element_type=F32)


def _split_bf16(x):
    hi = x.astype(BF16)
    lo = (x - hi.astype(F32)).astype(BF16)
    return hi, lo


def _rmsnorm(x, g):
    return x * lax.rsqrt(jnp.mean(x * x, axis=-1, keepdims=True) + EPS) * g


def _mixer_kernel(h_ref, nmix_ref, win_ref, wgate_ref, bgate_ref, gnorm_ref, wdw_ref, bdw_ref, lng_ref, lnb_ref,
                  bias_ref, wout_ref, later_ref, hmask_ref, hsum_ref, o_ref, khist, vhist, cbuf, st_ref):
    t = pl.program_id(1)

    @pl.when(t == 0)
    def _start_of_sequence():
        khist[...] = jnp.zeros_like(khist)
        vhist[...] = jnp.zeros_like(vhist)
        cbuf[0:CONV_PAD, :] = jnp.zeros((CONV_PAD, CONV_WIDTH), F32)
        st_ref[...] = jnp.zeros_like(st_ref)

    h = h_ref[...]
    xn = _rmsnorm(h, nmix_ref[...]).astype(BF16)

    def proj(name):
        lo, hi = SEG[name]
        return _dot(xn, win_ref[:, lo:hi])

    gq = (proj("gq") * (GLA_DK ** -0.5)).astype(BF16)
    gk = proj("gk")
    gv = proj("gv").astype(BF16)
    z = _dot(proj("lr").astype(BF16), wgate_ref[...]) + bgate_ref[...]
    log_a = (jnp.minimum(z, 0.0) - jnp.log1p(jnp.exp(-jnp.abs(z)))) / GLA_GATE_TAU
    la_hi, la_lo = _split_bf16(log_a)
    later = later_ref[...]
    rev = _dot(later, la_hi) + _dot(later, la_lo)
    kdec = (gk * jnp.exp(rev)).astype(BF16)
    chunk_log = rev + log_a
    hmask = hmask_ref[...]
    st = st_ref[...]
    o_chunks = []
    for c in range(NCH):
        r0 = c * CHUNK
        decay = jnp.exp(chunk_log[r0:r0 + 1, :])
        kv_t = _dot_tn(gv[r0:r0 + CHUNK], kdec[r0:r0 + CHUNK])
        st = st * decay + kv_t * hmask
        o_chunks.append(_dot_nt(gq[r0:r0 + CHUNK], st.astype(BF16)))
    st_ref[...] = st
    og = jnp.concatenate(o_chunks, axis=0)
    sq_hi, sq_lo = _split_bf16(og * og)
    hsum = hsum_ref[...]
    ms = (_dot(sq_hi, hsum) + _dot(sq_lo, hsum)) / GLA_DV
    gg = proj("gg")
    o_gla = og * lax.rsqrt(ms + EPS) * gnorm_ref[...] * (gg * jax.nn.sigmoid(gg))

    hg = proj("ca") * jax.nn.sigmoid(proj("cb"))
    cbuf[CONV_PAD:CONV_PAD + TS, :] = hg
    base = CONV_PAD - (CONV_KERNEL - 1)
    rows = []
    for rb in range(TS // CHUNK):
        acc = jnp.zeros((CHUNK, CONV_WIDTH), F32)
        for j in range(CONV_KERNEL):
            acc = acc + cbuf[pl.ds(rb * CHUNK + base + j, CHUNK), :] * wdw_ref[j:j + 1, :]
        rows.append(acc)
    cbuf[0:CONV_PAD, :] = hg[TS - CONV_PAD:TS, :]
    hc = jnp.concatenate(rows, axis=0) + bdw_ref[...]
    mu = jnp.mean(hc, axis=-1, keepdims=True)
    var = jnp.mean(jnp.square(hc - mu), axis=-1, keepdims=True)
    hc = (hc - mu) * lax.rsqrt(var + EPS) * lng_ref[...] + lnb_ref[...]
    o_conv = hc * jax.nn.sigmoid(hc)

    aq = proj("aq") * (ATT_HEAD_DIM ** -0.5)
    kb = proj("ak").astype(BF16)
    vb = proj("av").astype(BF16)
    kwin = jnp.concatenate([khist[...], kb], axis=0)
    vwin = jnp.concatenate([vhist[...], vb], axis=0)
    khist[...] = kb
    vhist[...] = vb
    lane = lax.broadcasted_iota(jnp.int32, (1, LANES), 1)
    low_half = lane < ATT_HEAD_DIM
    col = lax.broadcasted_iota(jnp.int32, (1, KW), 1)
    pair_out = []
    for p in range(ATT_HEADS // 2):
        ls = slice(p * LANES, (p + 1) * LANES)
        blocks = []
        for qb in range(TS // QB):
            q2 = aq[qb * QB:(qb + 1) * QB, ls]
            kk = kwin[qb * QB:qb * QB + KW, ls]
            vv = vwin[qb * QB:qb * QB + KW, ls]
            n_invalid = jnp.where(t == 0, HIST - qb * QB, 0)
            neg = jnp.where(col < n_invalid, NEG_INF, 0.0)
            outs = []
            for hh in range(2):
                mine = low_half if hh == 0 else jnp.logical_not(low_half)
                qm = jnp.where(mine, q2, 0.0).astype(BF16)
                s = _dot_nt(qm, kk) + bias_ref[2 * p + hh] + neg
                m = jnp.max(s, axis=-1, keepdims=True)
                e = jnp.exp(s - m)
                denom = jnp.sum(e, axis=-1, keepdims=True)
                outs.append(_dot(e.astype(BF16), vv) / denom)
            blocks.append(jnp.where(low_half, outs[0], outs[1]))
        pair_out.append(jnp.concatenate(blocks, axis=0))
    o_att = jnp.concatenate(pair_out, axis=1)

    mixed = jnp.concatenate([o_gla, o_conv, o_att], axis=1).astype(BF16)
    o_ref[...] = h + _dot(mixed, wout_ref[...])


def _ffn_kernel(h_ref, nffn_ref, wup_ref, wdn_ref, nfin_ref, o_ref, *, final_norm):
    h = h_ref[...]
    xn = _rmsnorm(h, nffn_ref[...]).astype(BF16)
    acc = h
    for c in range(D_FF // FF_CHUNK):
        u = jnp.maximum(_dot(xn, wup_ref[:, c * FF_CHUNK:(c + 1) * FF_CHUNK]), 0.0)
        acc = acc + _dot((u * u).astype(BF16), wdn_ref[c * FF_CHUNK:(c + 1) * FF_CHUNK, :])
    if final_norm:
        acc = _rmsnorm(acc, nfin_ref[...])
    o_ref[...] = acc


def _const_spec(shape):
    nd = len(shape)
    return pl.BlockSpec(shape, lambda *_: (0,) * nd, pipeline_mode=pl.Buffered(1))


def _mixer_constants():
    i = np.arange(TS)
    later = ((i[None, :] > i[:, None]) & (i[None, :] // CHUNK == i[:, None] // CHUNK))
    v = np.arange(GLA_WIDTH)
    k = np.arange(KEY_PAD)
    hmask = ((v[:, None] // GLA_DV) == (k[None, :] // GLA_DK)) & (k[None, :] < GLA_KEY_WIDTH)
    hsum = (v[:, None] // GLA_DV) == (v[None, :] // GLA_DV)
    return (jnp.asarray(later, BF16), jnp.asarray(hmask, F32), jnp.asarray(hsum, BF16))


def _bias_table(rel_bias):
    i = np.arange(QB)[:, None]
    c = np.arange(KW)[None, :]
    rel_idx = np.clip(i - c + HIST, -MAX_REL_DIST, MAX_REL_DIST) + MAX_REL_DIST
    band = (c // CHUNK >= i // CHUNK) & (c // CHUNK <= i // CHUNK + ATT_LEFT_CHUNKS)
    return jnp.where(band[None], rel_bias.astype(F32)[:, rel_idx], NEG_INF)


def _pad_cols(w, width):
    return jnp.pad(w, ((0, 0), (0, width - w.shape[1])))


def _pack_w_in(w_in):
    sizes = [s[1] for s in _SEGS]
    splits = np.cumsum(sizes)[:-1].tolist()
    parts = jnp.split(w_in, splits, axis=1)
    return jnp.concatenate([_pad_cols(p, s[2]) for p, s in zip(parts, _SEGS)], axis=1).astype(BF16)


def _mixer_layer(h, nmix, w_in, w_gate, b_gate, gnorm, w_dw, b_dw, ln_g, ln_b, rel_bias, w_out):
    B, S, D = h.shape
    later, hmask, hsum = _mixer_constants()
    win = _pack_w_in(w_in)
    wgate = jnp.pad(w_gate, ((0, LR_PAD - GLA_GATE_RANK), (0, KEY_PAD - GLA_KEY_WIDTH))).astype(BF16)
    bgate = _pad_cols(b_gate[None, :], KEY_PAD)
    bias = _bias_table(rel_bias)
    row = lambda a: a[None, :].astype(F32)
    args = (h, row(nmix), win, wgate, bgate, row(gnorm), w_dw.astype(F32), row(b_dw), row(ln_g), row(ln_b),
            bias, w_out.astype(BF16), later, hmask, hsum)
    in_specs = [pl.BlockSpec((None, TS, D), lambda b, t: (b, t, 0))] + [_const_spec(a.shape) for a in args[1:]]
    return pl.pallas_call(
        _mixer_kernel,
        grid=(B, S // TS),
        in_specs=in_specs,
        out_specs=pl.BlockSpec((None, TS, D), lambda b, t: (b, t, 0)),
        out_shape=jax.ShapeDtypeStruct((B, S, D), F32),
        scratch_shapes=[pltpu.VMEM((HIST, ATT_WIDTH), BF16), pltpu.VMEM((HIST, ATT_WIDTH), BF16),
                        pltpu.VMEM((CONV_PAD + TS, CONV_WIDTH), F32), pltpu.VMEM((GLA_WIDTH, KEY_PAD), F32)],
        compiler_params=pltpu.CompilerParams(dimension_semantics=("arbitrary", "arbitrary"),
                                             vmem_limit_bytes=VMEM_LIMIT_BYTES),
        name="mixer",
    )(*args)


def _ffn_layer(h2, nffn, w_up, w_down, nfin, final_norm):
    T, D = h2.shape
    row = lambda a: a[None, :].astype(F32)
    args = (h2, row(nffn), w_up.astype(BF16), w_down.astype(BF16), row(nfin))
    in_specs = [pl.BlockSpec((TM, D), lambda i: (i, 0))] + [_const_spec(a.shape) for a in args[1:]]
    return pl.pallas_call(
        functools.partial(_ffn_kernel, final_norm=final_norm),
        grid=(T // TM,),
        in_specs=in_specs,
        out_specs=pl.BlockSpec((TM, D), lambda i: (i, 0)),
        out_shape=jax.ShapeDtypeStruct((T, D), F32),
        compiler_params=pltpu.CompilerParams(dimension_semantics=("arbitrary",),
                                             vmem_limit_bytes=VMEM_LIMIT_BYTES),
        name="ffn",
    )(*args)


def kernel(x, norm_mix, w_in, w_gla_gate, b_gla_gate, gla_norm, w_dw, b_dw, conv_ln_g, conv_ln_b, rel_bias, w_out,
           norm_ffn, w_up, w_down, norm_final):
    B, S, D = x.shape
    assert (D, S % TS, (B * S) % TM) == (D_MODEL, 0, 0)
    h = x
    for l in range(DEPTH):
        h = _mixer_layer(h, norm_mix[l], w_in[l], w_gla_gate[l], b_gla_gate[l], gla_norm[l], w_dw[l], b_dw[l],
                         conv_ln_g[l], conv_ln_b[l], rel_bias[l], w_out[l])
        h = _ffn_layer(h.reshape(B * S, D), norm_ffn[l], w_up[l], w_down[l], norm_final,
                       final_norm=(l == DEPTH - 1)).reshape(B, S, D)
    return h
```

```python
import functools

import numpy as np
import jax
import jax.numpy as jnp
from jax import lax
from jax.experimental import pallas as pl
from jax.experimental.pallas import tpu as pltpu

D_MODEL = 1024
DEPTH = 2
CHUNK = 64
GLA_DV, GLA_DK, GLA_HEADS = 96, 48, 4
GLA_WIDTH = GLA_HEADS * GLA_DV
GLA_KEY_WIDTH = GLA_HEADS * GLA_DK
GLA_GATE_RANK = 16
GLA_GATE_TAU = 16.0
CONV_WIDTH = 256
CONV_KERNEL = 31
ATT_HEAD_DIM = 64
ATT_HEADS = 6
ATT_WIDTH = ATT_HEADS * ATT_HEAD_DIM
ATT_LEFT_CHUNKS = 8
MAX_REL_DIST = 128
D_FF = 4 * D_MODEL
EPS = 1e-6
NEG_INF = -1e30

LANES = 128
VMEM_LIMIT_BYTES = 56 * 1024 * 1024

TS = 512
NCH = TS // CHUNK
HIST = ATT_LEFT_CHUNKS * CHUNK
QB = 256
KW = QB + HIST
CONV_PAD = 32
TM = 512
FF_CHUNK = 1024

KEY_PAD = 256
LR_PAD = LANES
_SEGS = (("gq", GLA_KEY_WIDTH, KEY_PAD), ("gk", GLA_KEY_WIDTH, KEY_PAD), ("gv", GLA_WIDTH, GLA_WIDTH),
         ("gg", GLA_WIDTH, GLA_WIDTH), ("lr", GLA_GATE_RANK, LR_PAD), ("ca", CONV_WIDTH, CONV_WIDTH),
         ("cb", CONV_WIDTH, CONV_WIDTH), ("aq", ATT_WIDTH, ATT_WIDTH), ("ak", ATT_WIDTH, ATT_WIDTH),
         ("av", ATT_WIDTH, ATT_WIDTH))
SEG = {}
_off = 0
for _name, _w, _wp in _SEGS:
    SEG[_name] = (_off, _off + _wp)
    _off += _wp
D_IN_PAD = _off

F32 = jnp.float32
BF16 = jnp.bfloat16


def _dot(a, b):
    return jnp.dot(a, b, preferred_element_type=F32)


def _dot_nt(a, b):
    return lax.dot_general(a, b, (((1,), (1,)), ((), ())), preferred_element_type=F32)


def _dot_tn(a, b):
    return lax.dot_general(a, b, (((0,), (0,)), ((), ())), preferred_element_type=F32)


def _split_bf16(x):
    hi = x.astype(BF16)
    lo = (x - hi.astype(F32)).astype(BF16)
    return hi, lo


def _rmsnorm(x, g):
    return x * lax.rsqrt(jnp.mean(x * x, axis=-1, keepdims=True) + EPS) * g


def _mixer_kernel(h_ref, nmix_ref, win_ref, wgate_ref, bgate_ref, gnorm_ref, wdw_ref, bdw_ref, lng_ref, lnb_ref,
                  bias_ref, wout_ref, later_ref, hmask_ref, hsum_ref, o_ref, khist, vhist, cbuf, st_ref):
    t = pl.program_id(1)

    @pl.when(t == 0)
    def _start_of_sequence():
        khist[...] = jnp.zeros_like(khist)
        vhist[...] = jnp.zeros_like(vhist)
        cbuf[0:CONV_PAD, :] = jnp.zeros((CONV_PAD, CONV_WIDTH), F32)
        st_ref[...] = jnp.zeros_like(st_ref)

    h = h_ref[...]
    xn = _rmsnorm(h, nmix_ref[...]).astype(BF16)

    def proj(name):
        lo, hi = SEG[name]
        return _dot(xn, win_ref[:, lo:hi])

    gq = (proj("gq") * (GLA_DK ** -0.5)).astype(BF16)
    gk = proj("gk")
    gv = proj("gv").astype(BF16)
    z = _dot(proj("lr").astype(BF16), wgate_ref[...]) + bgate_ref[...]
    log_a = (jnp.minimum(z, 0.0) - jnp.log1p(jnp.exp(-jnp.abs(z)))) / GLA_GATE_TAU
    la_hi, la_lo = _split_bf16(log_a)
    later = later_ref[...]
    rev = _dot(later, la_hi) + _dot(later, la_lo)
    kdec = (gk * jnp.exp(rev)).astype(BF16)
    chunk_log = rev + log_a
    hmask = hmask_ref[...]
    st = st_ref[...]
    o_chunks = []
    for c in range(NCH):
        r0 = c * CHUNK
        decay = jnp.exp(chunk_log[r0:r0 + 1, :])
        kv_t = _dot_tn(gv[r0:r0 + CHUNK], kdec[r0:r0 + CHUNK])
        st = st * decay + kv_t * hmask
        o_chunks.append(_dot_nt(gq[r0:r0 + CHUNK], st.astype(BF16)))
    st_ref[...] = st
    og = jnp.concatenate(o_chunks, axis=0)
    sq_hi, sq_lo = _split_bf16(og * og)
    hsum = hsum_ref[...]
    ms = (_dot(sq_hi, hsum) + _dot(sq_lo, hsum)) / GLA_DV
    gg = proj("gg")
    o_gla = og * lax.rsqrt(ms + EPS) * gnorm_ref[...] * (gg * jax.nn.sigmoid(gg))

    hg = proj("ca") * jax.nn.sigmoid(proj("cb"))
    cbuf[CONV_PAD:CONV_PAD + TS, :] = hg
    base = CONV_PAD - (CONV_KERNEL - 1)
    rows = []
    for rb in range(TS // CHUNK):
        acc = jnp.zeros((CHUNK, CONV_WIDTH), F32)
        for j in range(CONV_KERNEL):
            acc = acc + cbuf[pl.ds(rb * CHUNK + base + j, CHUNK), :] * wdw_ref[j:j + 1, :]
        rows.append(acc)
    cbuf[0:CONV_PAD, :] = hg[TS - CONV_PAD:TS, :]
    hc = jnp.concatenate(rows, axis=0) + bdw_ref[...]
    mu = jnp.mean(hc, axis=-1, keepdims=True)
    var = jnp.mean(jnp.square(hc - mu), axis=-1, keepdims=True)
    hc = (hc - mu) * lax.rsqrt(var + EPS) * lng_ref[...] + lnb_ref[...]
    o_conv = hc * jax.nn.sigmoid(hc)

    aq = proj("aq") * (ATT_HEAD_DIM ** -0.5)
    kb = proj("ak").astype(BF16)
    vb = proj("av").astype(BF16)
    kwin = jnp.concatenate([khist[...], kb], axis=0)
    vwin = jnp.concatenate([vhist[...], vb], axis=0)
    khist[...] = kb
    vhist[...] = vb
    lane = lax.broadcasted_iota(jnp.int32, (1, LANES), 1)
    low_half = lane < ATT_HEAD_DIM
    col = lax.broadcasted_iota(jnp.int32, (1, KW), 1)
    pair_out = []
    for p in range(ATT_HEADS // 2):
        ls = slice(p * LANES, (p + 1) * LANES)
        blocks = []
        for qb in range(TS // QB):
            q2 = aq[qb * QB:(qb + 1) * QB, ls]
            kk = kwin[qb * QB:qb * QB + KW, ls]
            vv = vwin[qb * QB:qb * QB + KW, ls]
            n_invalid = jnp.where(t == 0, HIST - qb * QB, 0)
            neg = jnp.where(col < n_invalid, NEG_INF, 0.0)
            outs = []
            for hh in range(2):
                mine = low_half if hh == 0 else jnp.logical_not(low_half)
                qm = jnp.where(mine, q2, 0.0).astype(BF16)
                s = _dot_nt(qm, kk) + bias_ref[2 * p + hh] + neg
                m = jnp.max(s, axis=-1, keepdims=True)
                e = jnp.exp(s - m)
                denom = jnp.sum(e, axis=-1, keepdims=True)
                outs.append(_dot(e.astype(BF16), vv) / denom)
            blocks.append(jnp.where(low_half, outs[0], outs[1]))
        pair_out.append(jnp.concatenate(blocks, axis=0))
    o_att = jnp.concatenate(pair_out, axis=1)

    mixed = jnp.concatenate([o_gla, o_conv, o_att], axis=1).astype(BF16)
    o_ref[...] = h + _dot(mixed, wout_ref[...])


def _ffn_kernel(h_ref, nffn_ref, wup_ref, wdn_ref, nfin_ref, o_ref, *, final_norm):
    h = h_ref[...]
    xn = _rmsnorm(h, nffn_ref[...]).astype(BF16)
    acc = h
    for c in range(D_FF // FF_CHUNK):
        u = jnp.maximum(_dot(xn, wup_ref[:, c * FF_CHUNK:(c + 1) * FF_CHUNK]), 0.0)
        acc = acc + _dot((u * u).astype(BF16), wdn_ref[c * FF_CHUNK:(c + 1) * FF_CHUNK, :])
    if final_norm:
        acc = _rmsnorm(acc, nfin_ref[...])
    o_ref[...] = acc


def _const_spec(shape):
    nd = len(shape)
    return pl.BlockSpec(shape, lambda *_: (0,) * nd, pipeline_mode=pl.Buffered(1))


def _mixer_constants():
    i = np.arange(TS)
    later = ((i[None, :] > i[:, None]) & (i[None, :] // CHUNK == i[:, None] // CHUNK))
    v = np.arange(GLA_WIDTH)
    k = np.arange(KEY_PAD)
    hmask = ((v[:, None] // GLA_DV) == (k[None, :] // GLA_DK)) & (k[None, :] < GLA_KEY_WIDTH)
    hsum = (v[:, None] // GLA_DV) == (v[None, :] // GLA_DV)
    return (jnp.asarray(later, BF16), jnp.asarray(hmask, F32), jnp.asarray(hsum, BF16))


def _bias_table(rel_bias):
    i = np.arange(QB)[:, None]
    c = np.arange(KW)[None, :]
    band = (c // CHUNK >= i // CHUNK) & (c // CHUNK <= i // CHUNK + ATT_LEFT_CHUNKS)
    period = 1024
    u = np.arange(period)
    dist = np.where(u < KW, HIST - u, HIST + period - u)
    w_idx = np.clip(dist, -MAX_REL_DIST, MAX_REL_DIST) + MAX_REL_DIST
    assert KW + QB <= period and (w_idx[:385] == 256).all() and (w_idx[385:641] == np.arange(255, -1, -1)).all() \
        and (w_idx[641:768] == 0).all() and (w_idx[768:] == 256).all()
    rb = rel_bias.astype(F32)
    nh = rb.shape[0]
    w = jnp.concatenate([jnp.broadcast_to(rb[:, 256:257], (nh, 385)), rb[:, 255::-1],
                         jnp.broadcast_to(rb[:, 0:1], (nh, 127)), jnp.broadcast_to(rb[:, 256:257], (nh, 256))], axis=1)
    skew = jnp.tile(w, (1, QB))[:, :QB * (period - 1)].reshape(nh, QB, period - 1)[:, :, :KW]
    return jnp.where(band[None], skew, NEG_INF)


def _pad_cols(w, width):
    return jnp.pad(w, ((0, 0), (0, width - w.shape[1])))


def _pack_w_in(w_in):
    sizes = [s[1] for s in _SEGS]
    splits = np.cumsum(sizes)[:-1].tolist()
    parts = jnp.split(w_in, splits, axis=1)
    return jnp.concatenate([_pad_cols(p, s[2]) for p, s in zip(parts, _SEGS)], axis=1).astype(BF16)


def _mixer_layer(h, nmix, w_in, w_gate, b_gate, gnorm, w_dw, b_dw, ln_g, ln_b, rel_bias, w_out):
    B, S, D = h.shape
    later, hmask, hsum = _mixer_constants()
    win = _pack_w_in(w_in)
    wgate = jnp.pad(w_gate, ((0, LR_PAD - GLA_GATE_RANK), (0, KEY_PAD - GLA_KEY_WIDTH))).astype(BF16)
    bgate = _pad_cols(b_gate[None, :], KEY_PAD)
    bias = _bias_table(rel_bias)
    row = lambda a: a[None, :].astype(F32)
    args = (h, row(nmix), win, wgate, bgate, row(gnorm), w_dw.astype(F32), row(b_dw), row(ln_g), row(ln_b),
            bias, w_out.astype(BF16), later, hmask, hsum)
    in_specs = [pl.BlockSpec((None, TS, D), lambda b, t: (b, t, 0))] + [_const_spec(a.shape) for a in args[1:]]
    return pl.pallas_call(
        _mixer_kernel,
        grid=(B, S // TS),
        in_specs=in_specs,
        out_specs=pl.BlockSpec((None, TS, D), lambda b, t: (b, t, 0)),
        out_shape=jax.ShapeDtypeStruct((B, S, D), F32),
        scratch_shapes=[pltpu.VMEM((HIST, ATT_WIDTH), BF16), pltpu.VMEM((HIST, ATT_WIDTH), BF16),
                        pltpu.VMEM((CONV_PAD + TS, CONV_WIDTH), F32), pltpu.VMEM((GLA_WIDTH, KEY_PAD), F32)],
        compiler_params=pltpu.CompilerParams(dimension_semantics=("arbitrary", "arbitrary"),
                                             vmem_limit_bytes=VMEM_LIMIT_BYTES),
        name="mixer",
    )(*args)


def _ffn_layer(h2, nffn, w_up, w_down, nfin, final_norm):
    T, D = h2.shape
    row = lambda a: a[None, :].astype(F32)
    args = (h2, row(nffn), w_up.astype(BF16), w_down.astype(BF16), row(nfin))
    in_specs = [pl.BlockSpec((TM, D), lambda i: (i, 0))] + [_const_spec(a.shape) for a in args[1:]]
    return pl.pallas_call(
        functools.partial(_ffn_kernel, final_norm=final_norm),
        grid=(T // TM,),
        in_specs=in_specs,
        out_specs=pl.BlockSpec((TM, D), lambda i: (i, 0)),
        out_shape=jax.ShapeDtypeStruct((T, D), F32),
        compiler_params=pltpu.CompilerParams(dimension_semantics=("arbitrary",),
                                             vmem_limit_bytes=VMEM_LIMIT_BYTES),
        name="ffn",
    )(*args)


def kernel(x, norm_mix, w_in, w_gla_gate, b_gla_gate, gla_norm, w_dw, b_dw, conv_ln_g, conv_ln_b, rel_bias, w_out,
           norm_ffn, w_up, w_down, norm_final):
    B, S, D = x.shape
    assert (D, S % TS, (B * S) % TM) == (D_MODEL, 0, 0)
    h = x
    for l in range(DEPTH):
        h = _mixer_layer(h, norm_mix[l], w_in[l], w_gla_gate[l], b_gla_gate[l], gla_norm[l], w_dw[l], b_dw[l],
                         conv_ln_g[l], conv_ln_b[l], rel_bias[l], w_out[l])
        h = _ffn_layer(h.reshape(B * S, D), norm_ffn[l], w_up[l], w_down[l], norm_final,
                       final_norm=(l == DEPTH - 1)).reshape(B, S, D)
    return h
```

```python
import functools

import numpy as np
import jax
import jax.numpy as jnp
from jax import lax
from jax.experimental import pallas as pl
from jax.experimental.pallas import tpu as pltpu

D_MODEL = 1024
DEPTH = 2
CHUNK = 64
GLA_DV, GLA_DK, GLA_HEADS = 96, 48, 4
GLA_WIDTH = GLA_HEADS * GLA_DV
GLA_KEY_WIDTH = GLA_HEADS * GLA_DK
GLA_GATE_RANK = 16
GLA_GATE_TAU = 16.0
CONV_WIDTH = 256
CONV_KERNEL = 31
ATT_HEAD_DIM = 64
ATT_HEADS = 6
ATT_WIDTH = ATT_HEADS * ATT_HEAD_DIM
ATT_LEFT_CHUNKS = 8
MAX_REL_DIST = 128
D_FF = 4 * D_MODEL
EPS = 1e-6
NEG_INF = -1e30

LANES = 128
SUBLANES = 8
VMEM_LIMIT_BYTES = 56 * 1024 * 1024

TS = 512
NCH = TS // CHUNK
HIST = ATT_LEFT_CHUNKS * CHUNK
QB = 256
KW = QB + HIST
CONV_PAD = 32
SHIFT_ROWS = TS + CONV_PAD - SUBLANES
TM = 512
FF_CHUNK = 1024

KEY_PAD = 256
LR_PAD = LANES
_SEGS = (("gq", GLA_KEY_WIDTH, KEY_PAD), ("gk", GLA_KEY_WIDTH, KEY_PAD), ("gv", GLA_WIDTH, GLA_WIDTH),
         ("gg", GLA_WIDTH, GLA_WIDTH), ("lr", GLA_GATE_RANK, LR_PAD), ("ca", CONV_WIDTH, CONV_WIDTH),
         ("cb", CONV_WIDTH, CONV_WIDTH), ("aq", ATT_WIDTH, ATT_WIDTH), ("ak", ATT_WIDTH, ATT_WIDTH),
         ("av", ATT_WIDTH, ATT_WIDTH))
_PACK_ORDER = ("gq", "gk", "gv", "lr", "gg", "aq", "ca", "cb", "ak", "av")
IN_GROUP = 1024
SEG = {}
_off = 0
for _name in _PACK_ORDER:
    _wp = {s[0]: s[2] for s in _SEGS}[_name]
    assert _off // IN_GROUP == (_off + _wp - 1) // IN_GROUP
    SEG[_name] = (_off, _off + _wp)
    _off += _wp
D_IN_PAD = _off
assert D_IN_PAD % IN_GROUP == 0

F32 = jnp.float32
BF16 = jnp.bfloat16


def _dot(a, b):
    return jnp.dot(a, b, preferred_element_type=F32)


def _dot_nt(a, b):
    return lax.dot_general(a, b, (((1,), (1,)), ((), ())), preferred_element_type=F32)


def _dot_tn(a, b):
    return lax.dot_general(a, b, (((0,), (0,)), ((), ())), preferred_element_type=F32)


def _split_bf16(x):
    hi = x.astype(BF16)
    lo = (x - hi.astype(F32)).astype(BF16)
    return hi, lo


def _rmsnorm(x, g):
    return x * lax.rsqrt(jnp.mean(x * x, axis=-1, keepdims=True) + EPS) * g


def _mixer_kernel(h_ref, nmix_ref, win_ref, wgate_ref, bgate_ref, gnorm_ref, wdw_ref, bdw_ref, lng_ref, lnb_ref,
                  bias_ref, wout_ref, later_ref, hmask_ref, hsum_ref, o_ref, khist, vhist, ctail, cbuf, shifted, st_ref):
    t = pl.program_id(1)
    cur = lax.rem(t, 2)
    nxt = 1 - cur

    @pl.when(t == 0)
    def _start_of_sequence():
        khist[0] = jnp.zeros((HIST, ATT_WIDTH), BF16)
        vhist[0] = jnp.zeros((HIST, ATT_WIDTH), BF16)
        ctail[0] = jnp.zeros((CONV_PAD, CONV_WIDTH), F32)
        st_ref[...] = jnp.zeros_like(st_ref)

    h = h_ref[...]
    xn = _rmsnorm(h, nmix_ref[...]).astype(BF16)

    groups = [_dot(xn, win_ref[:, g * IN_GROUP:(g + 1) * IN_GROUP]) for g in range(D_IN_PAD // IN_GROUP)]

    def proj(name):
        lo, hi = SEG[name]
        return groups[lo // IN_GROUP][:, lo % IN_GROUP:(hi - 1) % IN_GROUP + 1]

    gq = (proj("gq") * (GLA_DK ** -0.5)).astype(BF16)
    gk = proj("gk")
    gv = proj("gv").astype(BF16)
    z = _dot(proj("lr").astype(BF16), wgate_ref[...]) + bgate_ref[...]
    log_a = (jnp.minimum(z, 0.0) - jnp.log1p(jnp.exp(-jnp.abs(z)))) / GLA_GATE_TAU
    la_hi, la_lo = _split_bf16(log_a)
    later = later_ref[...]
    rev = _dot(later, la_hi) + _dot(later, la_lo)
    kdec = (gk * jnp.exp(rev)).astype(BF16)
    chunk_log = rev + log_a
    hmask = hmask_ref[...]
    st = st_ref[...]
    o_chunks = []
    for c in range(NCH):
        r0 = c * CHUNK
        decay = jnp.exp(chunk_log[r0:r0 + 1, :])
        kv_t = _dot_tn(gv[r0:r0 + CHUNK], kdec[r0:r0 + CHUNK])
        st = st * decay + kv_t * hmask
        o_chunks.append(_dot_nt(gq[r0:r0 + CHUNK], st.astype(BF16)))
    st_ref[...] = st
    og = jnp.concatenate(o_chunks, axis=0)
    sq_hi, sq_lo = _split_bf16(og * og)
    hsum = hsum_ref[...]
    ms = (_dot(sq_hi, hsum) + _dot(sq_lo, hsum)) / GLA_DV
    gg = proj("gg")
    o_gla = og * lax.rsqrt(ms + EPS) * gnorm_ref[...] * (gg * jax.nn.sigmoid(gg))

    hg = proj("ca") * jax.nn.sigmoid(proj("cb"))
    cbuf[0:CONV_PAD, :] = ctail[cur]
    cbuf[CONV_PAD:CONV_PAD + TS, :] = hg
    ctail[nxt] = hg[TS - CONV_PAD:TS, :]
    for s in range(1, SUBLANES):
        shifted[s - 1] = cbuf[pl.ds(s, SHIFT_ROWS), :]
    base = CONV_PAD - (CONV_KERNEL - 1)
    rows = []
    for rb in range(TS // CHUNK):
        acc = jnp.zeros((CHUNK, CONV_WIDTH), F32)
        for j in range(CONV_KERNEL):
            a, s = divmod(base + j, SUBLANES)
            src = cbuf if s == 0 else shifted.at[s - 1]
            acc = acc + src[pl.ds(rb * CHUNK + a * SUBLANES, CHUNK), :] * wdw_ref[j:j + 1, :]
        rows.append(acc)
    hc = jnp.concatenate(rows, axis=0) + bdw_ref[...]
    mu = jnp.mean(hc, axis=-1, keepdims=True)
    var = jnp.mean(jnp.square(hc - mu), axis=-1, keepdims=True)
    hc = (hc - mu) * lax.rsqrt(var + EPS) * lng_ref[...] + lnb_ref[...]
    o_conv = hc * jax.nn.sigmoid(hc)

    aq = proj("aq") * (ATT_HEAD_DIM ** -0.5)
    kb = proj("ak").astype(BF16)
    vb = proj("av").astype(BF16)
    khist[nxt] = kb
    vhist[nxt] = vb
    lane = lax.broadcasted_iota(jnp.int32, (1, LANES), 1)
    low_half = lane < ATT_HEAD_DIM
    hist_neg = jnp.where(t == 0, NEG_INF, 0.0)
    pair_out = []
    for p in range(ATT_HEADS // 2):
        ls = slice(p * LANES, (p + 1) * LANES)
        blocks = []
        for qb in range(TS // QB):
            n_hist = HIST - qb * QB
            q2 = aq[qb * QB:(qb + 1) * QB, ls]
            k_hist = khist[cur, qb * QB:HIST, ls]
            v_hist = vhist[cur, qb * QB:HIST, ls]
            k_cur = kb[0:KW - n_hist, ls]
            v_cur = vb[0:KW - n_hist, ls]
            outs = []
            for hh in range(2):
                mine = low_half if hh == 0 else jnp.logical_not(low_half)
                qm = jnp.where(mine, q2, 0.0).astype(BF16)
                s = jnp.concatenate([_dot_nt(qm, k_hist) + hist_neg, _dot_nt(qm, k_cur)], axis=1)
                s = s + bias_ref[2 * p + hh]
                m = jnp.max(s, axis=-1, keepdims=True)
                e = jnp.exp(s - m)
                denom = jnp.sum(e, axis=-1, keepdims=True)
                eb = e.astype(BF16)
                pv = _dot(eb[:, :n_hist], v_hist) + _dot(eb[:, n_hist:], v_cur)
                outs.append(pv / denom)
            blocks.append(jnp.where(low_half, outs[0], outs[1]))
        pair_out.append(jnp.concatenate(blocks, axis=0))
    o_att = jnp.concatenate(pair_out, axis=1)

    mixed = jnp.concatenate([o_gla, o_conv, o_att], axis=1).astype(BF16)
    o_ref[...] = h + _dot(mixed, wout_ref[...])


def _ffn_kernel(h_ref, nffn_ref, wup_ref, wdn_ref, nfin_ref, o_ref, *, final_norm):
    h = h_ref[...]
    xn = _rmsnorm(h, nffn_ref[...]).astype(BF16)
    acc = h
    for c in range(D_FF // FF_CHUNK):
        u = jnp.maximum(_dot(xn, wup_ref[:, c * FF_CHUNK:(c + 1) * FF_CHUNK]), 0.0)
        acc = acc + _dot((u * u).astype(BF16), wdn_ref[c * FF_CHUNK:(c + 1) * FF_CHUNK, :])
    if final_norm:
        acc = _rmsnorm(acc, nfin_ref[...])
    o_ref[...] = acc


def _const_spec(shape):
    nd = len(shape)
    return pl.BlockSpec(shape, lambda *_: (0,) * nd, pipeline_mode=pl.Buffered(1))


def _mixer_constants():
    i = np.arange(TS)
    later = ((i[None, :] > i[:, None]) & (i[None, :] // CHUNK == i[:, None] // CHUNK))
    v = np.arange(GLA_WIDTH)
    k = np.arange(KEY_PAD)
    hmask = ((v[:, None] // GLA_DV) == (k[None, :] // GLA_DK)) & (k[None, :] < GLA_KEY_WIDTH)
    hsum = (v[:, None] // GLA_DV) == (v[None, :] // GLA_DV)
    return (jnp.asarray(later, BF16), jnp.asarray(hmask, F32), jnp.asarray(hsum, BF16))


def _bias_table(rel_bias):
    i = np.arange(QB)[:, None]
    c = np.arange(KW)[None, :]
    band = (c // CHUNK >= i // CHUNK) & (c // CHUNK <= i // CHUNK + ATT_LEFT_CHUNKS)
    period = 1024
    u = np.arange(period)
    dist = np.where(u < KW, HIST - u, HIST + period - u)
    w_idx = np.clip(dist, -MAX_REL_DIST, MAX_REL_DIST) + MAX_REL_DIST
    assert KW + QB <= period and (w_idx[:385] == 256).all() and (w_idx[385:641] == np.arange(255, -1, -1)).all() \
        and (w_idx[641:768] == 0).all() and (w_idx[768:] == 256).all()
    rb = rel_bias.astype(F32)
    nh = rb.shape[0]
    w = jnp.concatenate([jnp.broadcast_to(rb[:, 256:257], (nh, 385)), rb[:, 255::-1],
                         jnp.broadcast_to(rb[:, 0:1], (nh, 127)), jnp.broadcast_to(rb[:, 256:257], (nh, 256))], axis=1)
    skew = jnp.tile(w, (1, QB))[:, :QB * (period - 1)].reshape(nh, QB, period - 1)[:, :, :KW]
    return jnp.where(band[None], skew, NEG_INF)


def _pad_cols(w, width):
    return jnp.pad(w, ((0, 0), (0, width - w.shape[1])))


def _pack_w_in(w_in):
    sizes = [s[1] for s in _SEGS]
    splits = np.cumsum(sizes)[:-1].tolist()
    parts = {s[0]: _pad_cols(p, s[2]) for p, s in zip(jnp.split(w_in, splits, axis=1), _SEGS)}
    return jnp.concatenate([parts[name] for name in _PACK_ORDER], axis=1).astype(BF16)


def _mixer_layer(h, nmix, w_in, w_gate, b_gate, gnorm, w_dw, b_dw, ln_g, ln_b, rel_bias, w_out):
    B, S, D = h.shape
    later, hmask, hsum = _mixer_constants()
    win = _pack_w_in(w_in)
    wgate = jnp.pad(w_gate, ((0, LR_PAD - GLA_GATE_RANK), (0, KEY_PAD - GLA_KEY_WIDTH))).astype(BF16)
    bgate = _pad_cols(b_gate[None, :], KEY_PAD)
    bias = _bias_table(rel_bias)
    row = lambda a: a[None, :].astype(F32)
    args = (h, row(nmix), win, wgate, bgate, row(gnorm), w_dw.astype(F32), row(b_dw), row(ln_g), row(ln_b),
            bias, w_out.astype(BF16), later, hmask, hsum)
    in_specs = [pl.BlockSpec((None, TS, D), lambda b, t: (b, t, 0))] + [_const_spec(a.shape) for a in args[1:]]
    return pl.pallas_call(
        _mixer_kernel,
        grid=(B, S // TS),
        in_specs=in_specs,
        out_specs=pl.BlockSpec((None, TS, D), lambda b, t: (b, t, 0)),
        out_shape=jax.ShapeDtypeStruct((B, S, D), F32),
        scratch_shapes=[pltpu.VMEM((2, HIST, ATT_WIDTH), BF16), pltpu.VMEM((2, HIST, ATT_WIDTH), BF16),
                        pltpu.VMEM((2, CONV_PAD, CONV_WIDTH), F32),
                        pltpu.VMEM((CONV_PAD + TS, CONV_WIDTH), F32),
                        pltpu.VMEM((SUBLANES - 1, SHIFT_ROWS, CONV_WIDTH), F32),
                        pltpu.VMEM((GLA_WIDTH, KEY_PAD), F32)],
        compiler_params=pltpu.CompilerParams(dimension_semantics=("arbitrary", "arbitrary"),
                                             vmem_limit_bytes=VMEM_LIMIT_BYTES),
        name="mixer",
    )(*args)


def _ffn_layer(h2, nffn, w_up, w_down, nfin, final_norm):
    T, D = h2.shape
    row = lambda a: a[None, :].astype(F32)
    args = (h2, row(nffn), w_up.astype(BF16), w_down.astype(BF16), row(nfin))
    in_specs = [pl.BlockSpec((TM, D), lambda i: (i, 0))] + [_const_spec(a.shape) for a in args[1:]]
    return pl.pallas_call(
        functools.partial(_ffn_kernel, final_norm=final_norm),
        grid=(T // TM,),
        in_specs=in_specs,
        out_specs=pl.BlockSpec((TM, D), lambda i: (i, 0)),
        out_shape=jax.ShapeDtypeStruct((T, D), F32),
        compiler_params=pltpu.CompilerParams(dimension_semantics=("arbitrary",),
                                             vmem_limit_bytes=VMEM_LIMIT_BYTES),
        name="ffn",
    )(*args)


def kernel(x, norm_mix, w_in, w_gla_gate, b_gla_gate, gla_norm, w_dw, b_dw, conv_ln_g, conv_ln_b, rel_bias, w_out,
           norm_ffn, w_up, w_down, norm_final):
    B, S, D = x.shape
    assert (D, S % (2 * TS), (B * S) % TM) == (D_MODEL, 0, 0)
    h = x
    for l in range(DEPTH):
        h = _mixer_layer(h, norm_mix[l], w_in[l], w_gla_gate[l], b_gla_gate[l], gla_norm[l], w_dw[l], b_dw[l],
                         conv_ln_g[l], conv_ln_b[l], rel_bias[l], w_out[l])
        h = _ffn_layer(h.reshape(B * S, D), norm_ffn[l], w_up[l], w_down[l], norm_final,
                       final_norm=(l == DEPTH - 1)).reshape(B, S, D)
    return h
```

```python
import functools

import numpy as np
import jax
import jax.numpy as jnp
from jax import lax
from jax.experimental import pallas as pl
from jax.experimental.pallas import tpu as pltpu

D_MODEL = 1024
DEPTH = 2
CHUNK = 64
GLA_DV, GLA_DK, GLA_HEADS = 96, 48, 4
GLA_WIDTH = GLA_HEADS * GLA_DV
GLA_KEY_WIDTH = GLA_HEADS * GLA_DK
GLA_GATE_RANK = 16
GLA_GATE_TAU = 16.0
CONV_WIDTH = 256
CONV_KERNEL = 31
ATT_HEAD_DIM = 64
ATT_HEADS = 6
ATT_WIDTH = ATT_HEADS * ATT_HEAD_DIM
ATT_LEFT_CHUNKS = 8
MAX_REL_DIST = 128
D_FF = 4 * D_MODEL
EPS = 1e-6
LOG2E = 1.4426950408889634
NEG_INF = -1e30

LANES = 128
SUBLANES = 8
VMEM_LIMIT_BYTES = 56 * 1024 * 1024

TS = 512
NCH = TS // CHUNK
HIST = ATT_LEFT_CHUNKS * CHUNK
QB = 256
KW = QB + HIST
CONV_PAD = 32
SHIFT_ROWS = TS + CONV_PAD - SUBLANES
TM = 512
FF_CHUNK = 1024

KEY_PAD = 256
LR_PAD = LANES
_SEGS = (("gq", GLA_KEY_WIDTH, KEY_PAD), ("gk", GLA_KEY_WIDTH, KEY_PAD), ("gv", GLA_WIDTH, GLA_WIDTH),
         ("gg", GLA_WIDTH, GLA_WIDTH), ("lr", GLA_GATE_RANK, LR_PAD), ("ca", CONV_WIDTH, CONV_WIDTH),
         ("cb", CONV_WIDTH, CONV_WIDTH), ("aq", ATT_WIDTH, ATT_WIDTH), ("ak", ATT_WIDTH, ATT_WIDTH),
         ("av", ATT_WIDTH, ATT_WIDTH))
_PACK_ORDER = ("gq", "gk", "gv", "lr", "gg", "aq", "ca", "cb", "ak", "av")
IN_GROUP = 1024
SEG = {}
_off = 0
for _name in _PACK_ORDER:
    _wp = {s[0]: s[2] for s in _SEGS}[_name]
    assert _off // IN_GROUP == (_off + _wp - 1) // IN_GROUP
    SEG[_name] = (_off, _off + _wp)
    _off += _wp
D_IN_PAD = _off
assert D_IN_PAD % IN_GROUP == 0

F32 = jnp.float32
BF16 = jnp.bfloat16


def _dot(a, b):
    return jnp.dot(a, b, preferred_element_type=F32)


def _dot_nt(a, b):
    return lax.dot_general(a, b, (((1,), (1,)), ((), ())), preferred_element_type=F32)


def _dot_tn(a, b):
    return lax.dot_general(a, b, (((0,), (0,)), ((), ())), preferred_element_type=F32)


def _split_bf16(x):
    hi = x.astype(BF16)
    lo = (x - hi.astype(F32)).astype(BF16)
    return hi, lo


def _rmsnorm(x, g):
    return x * lax.rsqrt(jnp.mean(x * x, axis=-1, keepdims=True) + EPS) * g


def _mixer_kernel(h_ref, nmix_ref, win_ref, wgate_ref, bgate_ref, gnorm_ref, wdw_ref, bdw_ref, lng_ref, lnb_ref,
                  bias_ref, wout_ref, later_ref, hmask_ref, hsum_ref, o_ref, khist, vhist, ctail, cbuf, shifted, st_ref):
    t = pl.program_id(1)
    cur = lax.rem(t, 2)
    nxt = 1 - cur

    @pl.when(t == 0)
    def _start_of_sequence():
        khist[0] = jnp.zeros((HIST, ATT_WIDTH), BF16)
        vhist[0] = jnp.zeros((HIST, ATT_WIDTH), BF16)
        ctail[0] = jnp.zeros((CONV_PAD, CONV_WIDTH), F32)
        st_ref[...] = jnp.zeros_like(st_ref)

    h = h_ref[...]
    xn = _rmsnorm(h, nmix_ref[...]).astype(BF16)
    groups = [_dot(xn, win_ref[:, g * IN_GROUP:(g + 1) * IN_GROUP]) for g in range(D_IN_PAD // IN_GROUP)]

    def proj(name):
        lo, hi = SEG[name]
        return groups[lo // IN_GROUP][:, lo % IN_GROUP:(hi - 1) % IN_GROUP + 1]

    o_gla = _gla_mixer(proj, wgate_ref, bgate_ref, gnorm_ref, later_ref, hmask_ref, hsum_ref, st_ref)
    o_conv = _conv_mixer(proj, cur, nxt, wdw_ref, bdw_ref, lng_ref, lnb_ref, ctail, cbuf, shifted)
    o_att = _attention_mixer(proj, t, cur, nxt, bias_ref, khist, vhist)
    mixed = jnp.concatenate([o_gla, o_conv, o_att], axis=1).astype(BF16)
    o_ref[...] = h + _dot(mixed, wout_ref[...])


def _gla_mixer(proj, wgate_ref, bgate_ref, gnorm_ref, later_ref, hmask_ref, hsum_ref, st_ref):
    gq = (proj("gq") * (GLA_DK ** -0.5)).astype(BF16)
    gk = proj("gk")
    gv = proj("gv").astype(BF16)
    z = _dot(proj("lr").astype(BF16), wgate_ref[...]) + bgate_ref[...]
    log_a = (jnp.minimum(z, 0.0) - jnp.log1p(jnp.exp(-jnp.abs(z)))) / GLA_GATE_TAU
    la_hi, la_lo = _split_bf16(log_a)
    later = later_ref[...]
    rev = _dot(later, la_hi) + _dot(later, la_lo)
    kdec = (gk * jnp.exp(rev)).astype(BF16)
    chunk_log = rev + log_a
    hmask = hmask_ref[...]
    chunks = [slice(c * CHUNK, (c + 1) * CHUNK) for c in range(NCH)]
    kv_t = [_dot_tn(gv[rows], kdec[rows]) * hmask for rows in chunks]
    decay = [jnp.exp(chunk_log[rows.start:rows.start + 1, :]) for rows in chunks]
    st = st_ref[...]
    states = []
    for c in range(NCH):
        st = st * decay[c] + kv_t[c]
        states.append(st.astype(BF16))
    st_ref[...] = st
    o_chunks = [_dot_nt(gq[rows], states[c]) for c, rows in enumerate(chunks)]
    og = jnp.concatenate(o_chunks, axis=0)
    sq_hi, sq_lo = _split_bf16(og * og)
    hsum = hsum_ref[...]
    ms = (_dot(sq_hi, hsum) + _dot(sq_lo, hsum)) / GLA_DV
    gg = proj("gg")
    return og * lax.rsqrt(ms + EPS) * gnorm_ref[...] * (gg * jax.nn.sigmoid(gg))


def _conv_mixer(proj, cur, nxt, wdw_ref, bdw_ref, lng_ref, lnb_ref, ctail, cbuf, shifted):
    hg = proj("ca") * jax.nn.sigmoid(proj("cb"))
    cbuf[0:CONV_PAD, :] = ctail[cur]
    cbuf[CONV_PAD:CONV_PAD + TS, :] = hg
    ctail[nxt] = hg[TS - CONV_PAD:TS, :]
    for s in range(1, SUBLANES):
        shifted[s - 1] = cbuf[pl.ds(s, SHIFT_ROWS), :]
    base = CONV_PAD - (CONV_KERNEL - 1)
    rows = []
    for rb in range(TS // CHUNK):
        acc = jnp.zeros((CHUNK, CONV_WIDTH), F32)
        for j in range(CONV_KERNEL):
            a, s = divmod(base + j, SUBLANES)
            src = cbuf if s == 0 else shifted.at[s - 1]
            acc = acc + src[pl.ds(rb * CHUNK + a * SUBLANES, CHUNK), :] * wdw_ref[j:j + 1, :]
        rows.append(acc)
    hc = jnp.concatenate(rows, axis=0) + bdw_ref[...]
    mu = jnp.mean(hc, axis=-1, keepdims=True)
    var = jnp.mean(jnp.square(hc - mu), axis=-1, keepdims=True)
    hc = (hc - mu) * lax.rsqrt(var + EPS) * lng_ref[...] + lnb_ref[...]
    return hc * jax.nn.sigmoid(hc)


def _attention_mixer(proj, t, cur, nxt, bias_ref, khist, vhist):
    aq = proj("aq") * (ATT_HEAD_DIM ** -0.5 * LOG2E)
    kb = proj("ak").astype(BF16)
    vb = proj("av").astype(BF16)
    khist[nxt] = kb
    vhist[nxt] = vb
    lane = lax.broadcasted_iota(jnp.int32, (1, LANES), 1)
    low_half = lane < ATT_HEAD_DIM
    ones_lane = (ATT_HEAD_DIM, 0)
    hist_neg = jnp.where(t == 0, NEG_INF, 0.0)
    one = jnp.ones((), BF16)
    v_ones = {}
    for p in range(ATT_HEADS // 2):
        ls = slice(p * LANES, (p + 1) * LANES)
        for hh, ol in enumerate(ones_lane):
            v_ones[p, hh] = (jnp.where(lane == ol, one, vhist[cur, :, ls]), jnp.where(lane == ol, one, vb[:, ls]))

    blocks = [(p, qb, hh) for p in range(ATT_HEADS // 2) for qb in range(TS // QB) for hh in range(2)]
    live = [{} for _ in blocks]

    def scores(n):
        p, qb, hh = blocks[n]
        ls = slice(p * LANES, (p + 1) * LANES)
        n_hist = HIST - qb * QB
        mine = low_half if hh == 0 else jnp.logical_not(low_half)
        qm = jnp.where(mine, aq[qb * QB:(qb + 1) * QB, ls], 0.0).astype(BF16)
        k_hist = khist[cur, qb * QB:HIST, ls]
        k_cur = kb[0:KW - n_hist, ls]
        live[n]["s"] = jnp.concatenate([_dot_nt(qm, k_hist) + hist_neg, _dot_nt(qm, k_cur)], axis=1)

    def row_max(n):
        p, _, hh = blocks[n]
        s = live[n].pop("s") + bias_ref[2 * p + hh]
        live[n]["s"] = s
        live[n]["m"] = jnp.max(s, axis=-1, keepdims=True)

    def exponent(n):
        live[n]["e"] = jnp.exp2(live[n].pop("s") - live[n].pop("m")).astype(BF16)

    def weighted_values(n):
        p, qb, hh = blocks[n]
        n_hist = HIST - qb * QB
        v_hist, v_cur = v_ones[p, hh][0][qb * QB:HIST], v_ones[p, hh][1][0:KW - n_hist]
        e = live[n].pop("e")
        live[n]["pv"] = _dot(e[:, :n_hist], v_hist) + _dot(e[:, n_hist:], v_cur)

    def normalise(n):
        hh = blocks[n][2]
        pv = live[n].pop("pv")
        live[n]["o"] = pv / pv[:, ones_lane[hh]:ones_lane[hh] + 1]

    stages = (scores, row_max, exponent, weighted_values, normalise)
    for step in range(len(blocks) + len(stages) - 1):
        for depth, stage in enumerate(stages):
            if 0 <= step - depth < len(blocks):
                stage(step - depth)

    o = {blk: live[n]["o"] for n, blk in enumerate(blocks)}
    return jnp.concatenate(
        [jnp.concatenate([jnp.where(low_half, o[p, qb, 0], o[p, qb, 1]) for qb in range(TS // QB)], axis=0)
         for p in range(ATT_HEADS // 2)], axis=1)


def _ffn_kernel(h_ref, nffn_ref, wup_ref, wdn_ref, nfin_ref, o_ref, *, final_norm):
    h = h_ref[...]
    xn = _rmsnorm(h, nffn_ref[...]).astype(BF16)
    acc = h
    for c in range(D_FF // FF_CHUNK):
        u = jnp.maximum(_dot(xn, wup_ref[:, c * FF_CHUNK:(c + 1) * FF_CHUNK]), 0.0)
        acc = acc + _dot((u * u).astype(BF16), wdn_ref[c * FF_CHUNK:(c + 1) * FF_CHUNK, :])
    if final_norm:
        acc = _rmsnorm(acc, nfin_ref[...])
    o_ref[...] = acc


def _const_spec(shape):
    nd = len(shape)
    return pl.BlockSpec(shape, lambda *_: (0,) * nd, pipeline_mode=pl.Buffered(1))


def _mixer_constants():
    i = np.arange(TS)
    later = ((i[None, :] > i[:, None]) & (i[None, :] // CHUNK == i[:, None] // CHUNK))
    v = np.arange(GLA_WIDTH)
    k = np.arange(KEY_PAD)
    hmask = ((v[:, None] // GLA_DV) == (k[None, :] // GLA_DK)) & (k[None, :] < GLA_KEY_WIDTH)
    hsum = (v[:, None] // GLA_DV) == (v[None, :] // GLA_DV)
    return (jnp.asarray(later, BF16), jnp.asarray(hmask, F32), jnp.asarray(hsum, BF16))


def _bias_table(rel_bias):
    i = np.arange(QB)[:, None]
    c = np.arange(KW)[None, :]
    band = (c // CHUNK >= i // CHUNK) & (c // CHUNK <= i // CHUNK + ATT_LEFT_CHUNKS)
    period = 1024
    u = np.arange(period)
    dist = np.where(u < KW, HIST - u, HIST + period - u)
    w_idx = np.clip(dist, -MAX_REL_DIST, MAX_REL_DIST) + MAX_REL_DIST
    assert KW + QB <= period and (w_idx[:385] == 256).all() and (w_idx[385:641] == np.arange(255, -1, -1)).all() \
        and (w_idx[641:768] == 0).all() and (w_idx[768:] == 256).all()
    rb = rel_bias.astype(F32)
    nh = rb.shape[0]
    w = jnp.concatenate([jnp.broadcast_to(rb[:, 256:257], (nh, 385)), rb[:, 255::-1],
                         jnp.broadcast_to(rb[:, 0:1], (nh, 127)), jnp.broadcast_to(rb[:, 256:257], (nh, 256))], axis=1)
    skew = jnp.tile(w, (1, QB))[:, :QB * (period - 1)].reshape(nh, QB, period - 1)[:, :, :KW]
    return jnp.where(band[None], skew * LOG2E, NEG_INF)


def _pad_cols(w, width):
    return jnp.pad(w, ((0, 0), (0, width - w.shape[1])))


def _pack_w_in(w_in):
    sizes = [s[1] for s in _SEGS]
    splits = np.cumsum(sizes)[:-1].tolist()
    parts = {s[0]: _pad_cols(p, s[2]) for p, s in zip(jnp.split(w_in, splits, axis=1), _SEGS)}
    return jnp.concatenate([parts[name] for name in _PACK_ORDER], axis=1).astype(BF16)


def _mixer_layer(h, nmix, w_in, w_gate, b_gate, gnorm, w_dw, b_dw, ln_g, ln_b, rel_bias, w_out):
    B, S, D = h.shape
    later, hmask, hsum = _mixer_constants()
    win = _pack_w_in(w_in)
    wgate = jnp.pad(w_gate, ((0, LR_PAD - GLA_GATE_RANK), (0, KEY_PAD - GLA_KEY_WIDTH))).astype(BF16)
    bgate = _pad_cols(b_gate[None, :], KEY_PAD)
    bias = _bias_table(rel_bias)
    row = lambda a: a[None, :].astype(F32)
    args = (h, row(nmix), win, wgate, bgate, row(gnorm), w_dw.astype(F32), row(b_dw), row(ln_g), row(ln_b),
            bias, w_out.astype(BF16), later, hmask, hsum)
    in_specs = [pl.BlockSpec((None, TS, D), lambda b, t: (b, t, 0))] + [_const_spec(a.shape) for a in args[1:]]
    return pl.pallas_call(
        _mixer_kernel,
        grid=(B, S // TS),
        in_specs=in_specs,
        out_specs=pl.BlockSpec((None, TS, D), lambda b, t: (b, t, 0)),
        out_shape=jax.ShapeDtypeStruct((B, S, D), F32),
        scratch_shapes=[pltpu.VMEM((2, HIST, ATT_WIDTH), BF16), pltpu.VMEM((2, HIST, ATT_WIDTH), BF16),
                        pltpu.VMEM((2, CONV_PAD, CONV_WIDTH), F32),
                        pltpu.VMEM((CONV_PAD + TS, CONV_WIDTH), F32),
                        pltpu.VMEM((SUBLANES - 1, SHIFT_ROWS, CONV_WIDTH), F32),
                        pltpu.VMEM((GLA_WIDTH, KEY_PAD), F32)],
        compiler_params=pltpu.CompilerParams(dimension_semantics=("arbitrary", "arbitrary"),
                                             vmem_limit_bytes=VMEM_LIMIT_BYTES),
        name="mixer",
    )(*args)


def _ffn_layer(h2, nffn, w_up, w_down, nfin, final_norm):
    T, D = h2.shape
    row = lambda a: a[None, :].astype(F32)
    args = (h2, row(nffn), w_up.astype(BF16), w_down.astype(BF16), row(nfin))
    in_specs = [pl.BlockSpec((TM, D), lambda i: (i, 0))] + [_const_spec(a.shape) for a in args[1:]]
    return pl.pallas_call(
        functools.partial(_ffn_kernel, final_norm=final_norm),
        grid=(T // TM,),
        in_specs=in_specs,
        out_specs=pl.BlockSpec((TM, D), lambda i: (i, 0)),
        out_shape=jax.ShapeDtypeStruct((T, D), F32),
        compiler_params=pltpu.CompilerParams(dimension_semantics=("arbitrary",),
                                             vmem_limit_bytes=VMEM_LIMIT_BYTES),
        name="ffn",
    )(*args)


def kernel(x, norm_mix, w_in, w_gla_gate, b_gla_gate, gla_norm, w_dw, b_dw, conv_ln_g, conv_ln_b, rel_bias, w_out,
           norm_ffn, w_up, w_down, norm_final):
    B, S, D = x.shape
    assert (D, S % (2 * TS), (B * S) % TM) == (D_MODEL, 0, 0)
    h = x
    for l in range(DEPTH):
        h = _mixer_layer(h, norm_mix[l], w_in[l], w_gla_gate[l], b_gla_gate[l], gla_norm[l], w_dw[l], b_dw[l],
                         conv_ln_g[l], conv_ln_b[l], rel_bias[l], w_out[l])
        h = _ffn_layer(h.reshape(B * S, D), norm_ffn[l], w_up[l], w_down[l], norm_final,
                       final_norm=(l == DEPTH - 1)).reshape(B, S, D)
    return h
```

```python
import functools

import numpy as np
import jax
import jax.numpy as jnp
from jax import lax
from jax.experimental import pallas as pl
from jax.experimental.pallas import tpu as pltpu

D_MODEL = 1024
DEPTH = 2
CHUNK = 64
GLA_DV, GLA_DK, GLA_HEADS = 96, 48, 4
GLA_WIDTH = GLA_HEADS * GLA_DV
GLA_KEY_WIDTH = GLA_HEADS * GLA_DK
GLA_GATE_RANK = 16
GLA_GATE_TAU = 16.0
CONV_WIDTH = 256
CONV_KERNEL = 31
ATT_HEAD_DIM = 64
ATT_HEADS = 6
ATT_WIDTH = ATT_HEADS * ATT_HEAD_DIM
ATT_LEFT_CHUNKS = 8
MAX_REL_DIST = 128
D_FF = 4 * D_MODEL
EPS = 1e-6
LOG2E = 1.4426950408889634
NEG_INF = -1e30

LANES = 128
SUBLANES = 8
VMEM_LIMIT_BYTES = 56 * 1024 * 1024

TS = 512
NCH = TS // CHUNK
HIST = ATT_LEFT_CHUNKS * CHUNK
QB = 256
KW = QB + HIST
CONV_PAD = 32
SHIFT_ROWS = TS + CONV_PAD - SUBLANES
TM = 512
FF_CHUNK = 1024

KEY_PAD = 256
LR_PAD = LANES
_SEGS = (("gq", GLA_KEY_WIDTH, KEY_PAD), ("gk", GLA_KEY_WIDTH, KEY_PAD), ("gv", GLA_WIDTH, GLA_WIDTH),
         ("gg", GLA_WIDTH, GLA_WIDTH), ("lr", GLA_GATE_RANK, LR_PAD), ("ca", CONV_WIDTH, CONV_WIDTH),
         ("cb", CONV_WIDTH, CONV_WIDTH), ("aq", ATT_WIDTH, ATT_WIDTH), ("ak", ATT_WIDTH, ATT_WIDTH),
         ("av", ATT_WIDTH, ATT_WIDTH))
_PACK_ORDER = ("gq", "gk", "gv", "lr", "gg", "aq", "ca", "cb", "ak", "av")
IN_GROUP = 1024
SEG = {}
_off = 0
for _name in _PACK_ORDER:
    _wp = {s[0]: s[2] for s in _SEGS}[_name]
    assert _off // IN_GROUP == (_off + _wp - 1) // IN_GROUP
    SEG[_name] = (_off, _off + _wp)
    _off += _wp
D_IN_PAD = _off
assert D_IN_PAD % IN_GROUP == 0

F32 = jnp.float32
BF16 = jnp.bfloat16


def _dot(a, b):
    return jnp.dot(a, b, preferred_element_type=F32)


def _dot_nt(a, b):
    return lax.dot_general(a, b, (((1,), (1,)), ((), ())), preferred_element_type=F32)


def _dot_tn(a, b):
    return lax.dot_general(a, b, (((0,), (0,)), ((), ())), preferred_element_type=F32)


def _split_bf16(x):
    hi = x.astype(BF16)
    lo = (x - hi.astype(F32)).astype(BF16)
    return hi, lo


def _rmsnorm(x, g):
    return x * lax.rsqrt(jnp.mean(x * x, axis=-1, keepdims=True) + EPS) * g


def _mixer_kernel(h_ref, nmix_ref, win_ref, wgate_ref, bgate_ref, gnorm_ref, wdw_ref, bdw_ref, lng_ref, lnb_ref,
                  bias_ref, wout_ref, later_ref, hmask_ref, hsum_ref, o_ref, khist, vhist, ctail, cbuf, shifted, st_ref):
    t = pl.program_id(1)
    cur = lax.rem(t, 2)
    nxt = 1 - cur

    @pl.when(t == 0)
    def _start_of_sequence():
        khist[0] = jnp.zeros((HIST, ATT_WIDTH), BF16)
        vhist[0] = jnp.zeros((HIST, ATT_WIDTH), BF16)
        ctail[0] = jnp.zeros((CONV_PAD, CONV_WIDTH), F32)
        st_ref[...] = jnp.zeros_like(st_ref)

    h = h_ref[...]
    xn = _rmsnorm(h, nmix_ref[...]).astype(BF16)
    groups = {}

    def project(g):
        if g not in groups:
            groups[g] = _dot(xn, win_ref[:, g * IN_GROUP:(g + 1) * IN_GROUP])

    def proj(name):
        lo, hi = SEG[name]
        project(lo // IN_GROUP)
        return groups[lo // IN_GROUP][:, lo % IN_GROUP:(hi - 1) % IN_GROUP + 1]

    later_groups = [functools.partial(project, g) for g in range(1, D_IN_PAD // IN_GROUP)]
    o_gla = _gla_mixer(proj, wgate_ref, bgate_ref, gnorm_ref, later_ref, hmask_ref, hsum_ref, st_ref,
                       lambda: later_groups.pop(0)() if later_groups else None)
    o_conv = _conv_mixer(proj, cur, nxt, wdw_ref, bdw_ref, lng_ref, lnb_ref, ctail, cbuf, shifted)
    o_att = _attention_mixer(proj, t, cur, nxt, bias_ref, khist, vhist)
    mixed = jnp.concatenate([o_gla, o_conv, o_att], axis=1).astype(BF16)
    o_ref[...] = h + _dot(mixed, wout_ref[...])


def _gla_mixer(proj, wgate_ref, bgate_ref, gnorm_ref, later_ref, hmask_ref, hsum_ref, st_ref, between_links):
    gq = (proj("gq") * (GLA_DK ** -0.5)).astype(BF16)
    gk = proj("gk")
    gv = proj("gv").astype(BF16)
    z = _dot(proj("lr").astype(BF16), wgate_ref[...]) + bgate_ref[...]
    between_links()
    log_a = (jnp.minimum(z, 0.0) - jnp.log1p(jnp.exp(-jnp.abs(z)))) / GLA_GATE_TAU
    la_hi, la_lo = _split_bf16(log_a)
    later = later_ref[...]
    rev = _dot(later, la_hi) + _dot(later, la_lo)
    between_links()
    kdec = (gk * jnp.exp(rev)).astype(BF16)
    chunk_log = rev + log_a
    hmask = hmask_ref[...]
    chunks = [slice(c * CHUNK, (c + 1) * CHUNK) for c in range(NCH)]
    kv_t = [_dot_tn(gv[rows], kdec[rows]) * hmask for rows in chunks]
    decay = [jnp.exp(chunk_log[rows.start:rows.start + 1, :]) for rows in chunks]
    st = st_ref[...]
    states = []
    for c in range(NCH):
        st = st * decay[c] + kv_t[c]
        states.append(st.astype(BF16))
    st_ref[...] = st
    o_chunks = [_dot_nt(gq[rows], states[c]) for c, rows in enumerate(chunks)]
    og = jnp.concatenate(o_chunks, axis=0)
    sq_hi, sq_lo = _split_bf16(og * og)
    hsum = hsum_ref[...]
    ms = (_dot(sq_hi, hsum) + _dot(sq_lo, hsum)) / GLA_DV
    gg = proj("gg")
    return og * lax.rsqrt(ms + EPS) * gnorm_ref[...] * (gg * jax.nn.sigmoid(gg))


def _conv_mixer(proj, cur, nxt, wdw_ref, bdw_ref, lng_ref, lnb_ref, ctail, cbuf, shifted):
    hg = proj("ca") * jax.nn.sigmoid(proj("cb"))
    cbuf[0:CONV_PAD, :] = ctail[cur]
    cbuf[CONV_PAD:CONV_PAD + TS, :] = hg
    ctail[nxt] = hg[TS - CONV_PAD:TS, :]
    for s in range(1, SUBLANES):
        shifted[s - 1] = cbuf[pl.ds(s, SHIFT_ROWS), :]
    base = CONV_PAD - (CONV_KERNEL - 1)
    rows = []
    for rb in range(TS // CHUNK):
        acc = jnp.zeros((CHUNK, CONV_WIDTH), F32)
        for j in range(CONV_KERNEL):
            a, s = divmod(base + j, SUBLANES)
            src = cbuf if s == 0 else shifted.at[s - 1]
            acc = acc + src[pl.ds(rb * CHUNK + a * SUBLANES, CHUNK), :] * wdw_ref[j:j + 1, :]
        rows.append(acc)
    hc = jnp.concatenate(rows, axis=0) + bdw_ref[...]
    mu = jnp.mean(hc, axis=-1, keepdims=True)
    var = jnp.mean(jnp.square(hc - mu), axis=-1, keepdims=True)
    hc = (hc - mu) * lax.rsqrt(var + EPS) * lng_ref[...] + lnb_ref[...]
    return hc * jax.nn.sigmoid(hc)


def _attention_mixer(proj, t, cur, nxt, bias_ref, khist, vhist):
    aq = proj("aq") * (ATT_HEAD_DIM ** -0.5 * LOG2E)
    kb = proj("ak").astype(BF16)
    vb = proj("av").astype(BF16)
    khist[nxt] = kb
    vhist[nxt] = vb
    lane = lax.broadcasted_iota(jnp.int32, (1, LANES), 1)
    low_half = lane < ATT_HEAD_DIM
    ones_lane = (ATT_HEAD_DIM, 0)
    hist_neg = jnp.where(t == 0, NEG_INF, 0.0)
    one = jnp.ones((), BF16)
    v_ones = {}
    for p in range(ATT_HEADS // 2):
        ls = slice(p * LANES, (p + 1) * LANES)
        for hh, ol in enumerate(ones_lane):
            v_ones[p, hh] = (jnp.where(lane == ol, one, vhist[cur, :, ls]), jnp.where(lane == ol, one, vb[:, ls]))

    blocks = [(p, qb, hh) for p in range(ATT_HEADS // 2) for qb in range(TS // QB) for hh in range(2)]
    live = [{} for _ in blocks]

    def scores(n):
        p, qb, hh = blocks[n]
        ls = slice(p * LANES, (p + 1) * LANES)
        n_hist = HIST - qb * QB
        mine = low_half if hh == 0 else jnp.logical_not(low_half)
        qm = jnp.where(mine, aq[qb * QB:(qb + 1) * QB, ls], 0.0).astype(BF16)
        k_hist = khist[cur, qb * QB:HIST, ls]
        k_cur = kb[0:KW - n_hist, ls]
        live[n]["s"] = jnp.concatenate([_dot_nt(qm, k_hist) + hist_neg, _dot_nt(qm, k_cur)], axis=1)

    def row_max(n):
        p, _, hh = blocks[n]
        s = live[n].pop("s") + bias_ref[2 * p + hh]
        live[n]["s"] = s
        live[n]["m"] = jnp.max(s, axis=-1, keepdims=True)

    def exponent(n):
        live[n]["e"] = jnp.exp2(live[n].pop("s") - live[n].pop("m")).astype(BF16)

    def weighted_values(n):
        p, qb, hh = blocks[n]
        n_hist = HIST - qb * QB
        v_hist, v_cur = v_ones[p, hh][0][qb * QB:HIST], v_ones[p, hh][1][0:KW - n_hist]
        e = live[n].pop("e")
        live[n]["pv"] = _dot(e[:, :n_hist], v_hist) + _dot(e[:, n_hist:], v_cur)

    def normalise(n):
        hh = blocks[n][2]
        pv = live[n].pop("pv")
        live[n]["o"] = pv / pv[:, ones_lane[hh]:ones_lane[hh] + 1]

    stages = (scores, row_max, exponent, weighted_values, normalise)
    for step in range(len(blocks) + len(stages) - 1):
        for depth, stage in enumerate(stages):
            if 0 <= step - depth < len(blocks):
                stage(step - depth)

    o = {blk: live[n]["o"] for n, blk in enumerate(blocks)}
    return jnp.concatenate(
        [jnp.concatenate([jnp.where(low_half, o[p, qb, 0], o[p, qb, 1]) for qb in range(TS // QB)], axis=0)
         for p in range(ATT_HEADS // 2)], axis=1)


def _ffn_kernel(h_ref, nffn_ref, wup_ref, wdn_ref, nfin_ref, o_ref, *, final_norm):
    h = h_ref[...]
    xn = _rmsnorm(h, nffn_ref[...]).astype(BF16)
    acc = h
    for c in range(D_FF // FF_CHUNK):
        u = jnp.maximum(_dot(xn, wup_ref[:, c * FF_CHUNK:(c + 1) * FF_CHUNK]), 0.0)
        acc = acc + _dot((u * u).astype(BF16), wdn_ref[c * FF_CHUNK:(c + 1) * FF_CHUNK, :])
    if final_norm:
        acc = _rmsnorm(acc, nfin_ref[...])
    o_ref[...] = acc


def _const_spec(shape):
    nd = len(shape)
    return pl.BlockSpec(shape, lambda *_: (0,) * nd, pipeline_mode=pl.Buffered(1))


def _mixer_constants():
    i = np.arange(TS)
    later = ((i[None, :] > i[:, None]) & (i[None, :] // CHUNK == i[:, None] // CHUNK))
    v = np.arange(GLA_WIDTH)
    k = np.arange(KEY_PAD)
    hmask = ((v[:, None] // GLA_DV) == (k[None, :] // GLA_DK)) & (k[None, :] < GLA_KEY_WIDTH)
    hsum = (v[:, None] // GLA_DV) == (v[None, :] // GLA_DV)
    return (jnp.asarray(later, BF16), jnp.asarray(hmask, F32), jnp.asarray(hsum, BF16))


def _bias_table(rel_bias):
    i = np.arange(QB)[:, None]
    c = np.arange(KW)[None, :]
    band = (c // CHUNK >= i // CHUNK) & (c // CHUNK <= i // CHUNK + ATT_LEFT_CHUNKS)
    period = 1024
    u = np.arange(period)
    dist = np.where(u < KW, HIST - u, HIST + period - u)
    w_idx = np.clip(dist, -MAX_REL_DIST, MAX_REL_DIST) + MAX_REL_DIST
    assert KW + QB <= period and (w_idx[:385] == 256).all() and (w_idx[385:641] == np.arange(255, -1, -1)).all() \
        and (w_idx[641:768] == 0).all() and (w_idx[768:] == 256).all()
    rb = rel_bias.astype(F32)
    nh = rb.shape[0]
    w = jnp.concatenate([jnp.broadcast_to(rb[:, 256:257], (nh, 385)), rb[:, 255::-1],
                         jnp.broadcast_to(rb[:, 0:1], (nh, 127)), jnp.broadcast_to(rb[:, 256:257], (nh, 256))], axis=1)
    skew = jnp.tile(w, (1, QB))[:, :QB * (period - 1)].reshape(nh, QB, period - 1)[:, :, :KW]
    return jnp.where(band[None], skew * LOG2E, NEG_INF)


def _pad_cols(w, width):
    return jnp.pad(w, ((0, 0), (0, width - w.shape[1])))


def _pack_w_in(w_in):
    sizes = [s[1] for s in _SEGS]
    splits = np.cumsum(sizes)[:-1].tolist()
    parts = {s[0]: _pad_cols(p, s[2]) for p, s in zip(jnp.split(w_in, splits, axis=1), _SEGS)}
    return jnp.concatenate([parts[name] for name in _PACK_ORDER], axis=1).astype(BF16)


def _mixer_layer(h, nmix, w_in, w_gate, b_gate, gnorm, w_dw, b_dw, ln_g, ln_b, rel_bias, w_out):
    B, S, D = h.shape
    later, hmask, hsum = _mixer_constants()
    win = _pack_w_in(w_in)
    wgate = jnp.pad(w_gate, ((0, LR_PAD - GLA_GATE_RANK), (0, KEY_PAD - GLA_KEY_WIDTH))).astype(BF16)
    bgate = _pad_cols(b_gate[None, :], KEY_PAD)
    bias = _bias_table(rel_bias)
    row = lambda a: a[None, :].astype(F32)
    args = (h, row(nmix), win, wgate, bgate, row(gnorm), w_dw.astype(F32), row(b_dw), row(ln_g), row(ln_b),
            bias, w_out.astype(BF16), later, hmask, hsum)
    in_specs = [pl.BlockSpec((None, TS, D), lambda b, t: (b, t, 0))] + [_const_spec(a.shape) for a in args[1:]]
    return pl.pallas_call(
        _mixer_kernel,
        grid=(B, S // TS),
        in_specs=in_specs,
        out_specs=pl.BlockSpec((None, TS, D), lambda b, t: (b, t, 0)),
        out_shape=jax.ShapeDtypeStruct((B, S, D), F32),
        scratch_shapes=[pltpu.VMEM((2, HIST, ATT_WIDTH), BF16), pltpu.VMEM((2, HIST, ATT_WIDTH), BF16),
                        pltpu.VMEM((2, CONV_PAD, CONV_WIDTH), F32),
                        pltpu.VMEM((CONV_PAD + TS, CONV_WIDTH), F32),
                        pltpu.VMEM((SUBLANES - 1, SHIFT_ROWS, CONV_WIDTH), F32),
                        pltpu.VMEM((GLA_WIDTH, KEY_PAD), F32)],
        compiler_params=pltpu.CompilerParams(dimension_semantics=("arbitrary", "arbitrary"),
                                             vmem_limit_bytes=VMEM_LIMIT_BYTES),
        name="mixer",
    )(*args)


def _ffn_layer(h2, nffn, w_up, w_down, nfin, final_norm):
    T, D = h2.shape
    row = lambda a: a[None, :].astype(F32)
    args = (h2, row(nffn), w_up.astype(BF16), w_down.astype(BF16), row(nfin))
    in_specs = [pl.BlockSpec((TM, D), lambda i: (i, 0))] + [_const_spec(a.shape) for a in args[1:]]
    return pl.pallas_call(
        functools.partial(_ffn_kernel, final_norm=final_norm),
        grid=(T // TM,),
        in_specs=in_specs,
        out_specs=pl.BlockSpec((TM, D), lambda i: (i, 0)),
        out_shape=jax.ShapeDtypeStruct((T, D), F32),
        compiler_params=pltpu.CompilerParams(dimension_semantics=("arbitrary",),
                                             vmem_limit_bytes=VMEM_LIMIT_BYTES),
        name="ffn",
    )(*args)


def kernel(x, norm_mix, w_in, w_gla_gate, b_gla_gate, gla_norm, w_dw, b_dw, conv_ln_g, conv_ln_b, rel_bias, w_out,
           norm_ffn, w_up, w_down, norm_final):
    B, S, D = x.shape
    assert (D, S % (2 * TS), (B * S) % TM) == (D_MODEL, 0, 0)
    h = x
    for l in range(DEPTH):
        h = _mixer_layer(h, norm_mix[l], w_in[l], w_gla_gate[l], b_gla_gate[l], gla_norm[l], w_dw[l], b_dw[l],
                         conv_ln_g[l], conv_ln_b[l], rel_bias[l], w_out[l])
        h = _ffn_layer(h.reshape(B * S, D), norm_ffn[l], w_up[l], w_down[l], norm_final,
                       final_norm=(l == DEPTH - 1)).reshape(B, S, D)
    return h
```

```python
import functools

import numpy as np
import jax
import jax.numpy as jnp
from jax import lax
from jax.experimental import pallas as pl
from jax.experimental.pallas import tpu as pltpu

D_MODEL = 1024
DEPTH = 2
CHUNK = 64
GLA_DV, GLA_DK, GLA_HEADS = 96, 48, 4
GLA_WIDTH = GLA_HEADS * GLA_DV
GLA_KEY_WIDTH = GLA_HEADS * GLA_DK
GLA_GATE_RANK = 16
GLA_GATE_TAU = 16.0
CONV_WIDTH = 256
CONV_KERNEL = 31
ATT_HEAD_DIM = 64
ATT_HEADS = 6
ATT_WIDTH = ATT_HEADS * ATT_HEAD_DIM
ATT_LEFT_CHUNKS = 8
MAX_REL_DIST = 128
D_FF = 4 * D_MODEL
EPS = 1e-6
LOG2E = 1.4426950408889634
NEG_INF = -1e30

LANES = 128
SUBLANES = 8
VMEM_LIMIT_BYTES = 56 * 1024 * 1024

TS = 512
NCH = TS // CHUNK
HIST = ATT_LEFT_CHUNKS * CHUNK
QB = 256
KW = QB + HIST
CONV_PAD = 32
SHIFT_ROWS = TS + CONV_PAD - SUBLANES
TM = 1024
FF_CHUNK = 1024

KEY_PAD = 256
LR_PAD = LANES
_SEGS = (("gq", GLA_KEY_WIDTH, KEY_PAD), ("gk", GLA_KEY_WIDTH, KEY_PAD), ("gv", GLA_WIDTH, GLA_WIDTH),
         ("gg", GLA_WIDTH, GLA_WIDTH), ("lr", GLA_GATE_RANK, LR_PAD), ("ca", CONV_WIDTH, CONV_WIDTH),
         ("cb", CONV_WIDTH, CONV_WIDTH), ("aq", ATT_WIDTH, ATT_WIDTH), ("ak", ATT_WIDTH, ATT_WIDTH),
         ("av", ATT_WIDTH, ATT_WIDTH))
_PACK_ORDER = ("gq", "gk", "gv", "lr", "gg", "aq", "ca", "cb", "ak", "av")
IN_GROUP = 1024
SEG = {}
_off = 0
for _name in _PACK_ORDER:
    _wp = {s[0]: s[2] for s in _SEGS}[_name]
    assert _off // IN_GROUP == (_off + _wp - 1) // IN_GROUP
    SEG[_name] = (_off, _off + _wp)
    _off += _wp
D_IN_PAD = _off
assert D_IN_PAD % IN_GROUP == 0

F32 = jnp.float32
BF16 = jnp.bfloat16


def _dot(a, b):
    return jnp.dot(a, b, preferred_element_type=F32)


def _dot_nt(a, b):
    return lax.dot_general(a, b, (((1,), (1,)), ((), ())), preferred_element_type=F32)


def _dot_tn(a, b):
    return lax.dot_general(a, b, (((0,), (0,)), ((), ())), preferred_element_type=F32)


def _split_bf16(x):
    hi = x.astype(BF16)
    lo = (x - hi.astype(F32)).astype(BF16)
    return hi, lo


def _rmsnorm(x, g):
    return x * lax.rsqrt(jnp.mean(x * x, axis=-1, keepdims=True) + EPS) * g


def _mixer_kernel(h_ref, nmix_ref, win_ref, wgate_ref, bgate_ref, gnorm_ref, wdw_ref, bdw_ref, lng_ref, lnb_ref,
                  bias_ref, wout_ref, later_ref, hmask_ref, hsum_ref, o_ref, khist, vhist, ctail, cbuf, shifted, st_ref):
    t = pl.program_id(1)
    cur = lax.rem(t, 2)
    nxt = 1 - cur

    @pl.when(t == 0)
    def _start_of_sequence():
        khist[0] = jnp.zeros((HIST, ATT_WIDTH), BF16)
        vhist[0] = jnp.zeros((HIST, ATT_WIDTH), BF16)
        ctail[0] = jnp.zeros((CONV_PAD, CONV_WIDTH), F32)
        st_ref[...] = jnp.zeros_like(st_ref)

    h = h_ref[...]
    xn = _rmsnorm(h, nmix_ref[...]).astype(BF16)
    groups = {}

    def project(g):
        if g not in groups:
            groups[g] = _dot(xn, win_ref[:, g * IN_GROUP:(g + 1) * IN_GROUP])

    def proj(name):
        lo, hi = SEG[name]
        project(lo // IN_GROUP)
        return groups[lo // IN_GROUP][:, lo % IN_GROUP:(hi - 1) % IN_GROUP + 1]

    later_groups = [functools.partial(project, g) for g in range(1, D_IN_PAD // IN_GROUP)]
    o_gla = _gla_mixer(proj, wgate_ref, bgate_ref, gnorm_ref, later_ref, hmask_ref, hsum_ref, st_ref,
                       lambda: later_groups.pop(0)() if later_groups else None)
    o_conv = _conv_mixer(proj, cur, nxt, wdw_ref, bdw_ref, lng_ref, lnb_ref, ctail, cbuf, shifted)
    o_att = _attention_mixer(proj, t, cur, nxt, bias_ref, khist, vhist)
    mixed = jnp.concatenate([o_gla, o_conv, o_att], axis=1).astype(BF16)
    o_ref[...] = h + _dot(mixed, wout_ref[...])


def _gla_mixer(proj, wgate_ref, bgate_ref, gnorm_ref, later_ref, hmask_ref, hsum_ref, st_ref, between_links):
    gq = (proj("gq") * (GLA_DK ** -0.5)).astype(BF16)
    gk = proj("gk")
    gv = proj("gv").astype(BF16)
    z = _dot(proj("lr").astype(BF16), wgate_ref[...]) + bgate_ref[...]
    between_links()
    log_a = (jnp.minimum(z, 0.0) - jnp.log1p(jnp.exp(-jnp.abs(z)))) / GLA_GATE_TAU
    la_hi, la_lo = _split_bf16(log_a)
    later = later_ref[...]
    rev = _dot(later, la_hi) + _dot(later, la_lo)
    between_links()
    kdec = (gk * jnp.exp(rev)).astype(BF16)
    chunk_log = rev + log_a
    hmask = hmask_ref[...]
    chunks = [slice(c * CHUNK, (c + 1) * CHUNK) for c in range(NCH)]
    kv_t = [_dot_tn(gv[rows], kdec[rows]) * hmask for rows in chunks]
    decay = [jnp.exp(chunk_log[rows.start:rows.start + 1, :]) for rows in chunks]
    st = st_ref[...]
    states = []
    for c in range(NCH):
        st = st * decay[c] + kv_t[c]
        states.append(st.astype(BF16))
    st_ref[...] = st
    o_chunks = [_dot_nt(gq[rows], states[c]) for c, rows in enumerate(chunks)]
    og = jnp.concatenate(o_chunks, axis=0)
    sq_hi, sq_lo = _split_bf16(og * og)
    hsum = hsum_ref[...]
    ms = (_dot(sq_hi, hsum) + _dot(sq_lo, hsum)) / GLA_DV
    gg = proj("gg")
    return og * lax.rsqrt(ms + EPS) * gnorm_ref[...] * (gg * jax.nn.sigmoid(gg))


def _conv_mixer(proj, cur, nxt, wdw_ref, bdw_ref, lng_ref, lnb_ref, ctail, cbuf, shifted):
    hg = proj("ca") * jax.nn.sigmoid(proj("cb"))
    cbuf[0:CONV_PAD, :] = ctail[cur]
    cbuf[CONV_PAD:CONV_PAD + TS, :] = hg
    ctail[nxt] = hg[TS - CONV_PAD:TS, :]
    for s in range(1, SUBLANES):
        shifted[s - 1] = cbuf[pl.ds(s, SHIFT_ROWS), :]
    base = CONV_PAD - (CONV_KERNEL - 1)
    rows = []
    for rb in range(TS // CHUNK):
        acc = jnp.zeros((CHUNK, CONV_WIDTH), F32)
        for j in range(CONV_KERNEL):
            a, s = divmod(base + j, SUBLANES)
            src = cbuf if s == 0 else shifted.at[s - 1]
            acc = acc + src[pl.ds(rb * CHUNK + a * SUBLANES, CHUNK), :] * wdw_ref[j:j + 1, :]
        rows.append(acc)
    hc = jnp.concatenate(rows, axis=0) + bdw_ref[...]
    mu = jnp.mean(hc, axis=-1, keepdims=True)
    var = jnp.mean(jnp.square(hc - mu), axis=-1, keepdims=True)
    hc = (hc - mu) * lax.rsqrt(var + EPS) * lng_ref[...] + lnb_ref[...]
    return hc * jax.nn.sigmoid(hc)


def _attention_mixer(proj, t, cur, nxt, bias_ref, khist, vhist):
    aq = proj("aq") * (ATT_HEAD_DIM ** -0.5 * LOG2E)
    kb = proj("ak").astype(BF16)
    vb = proj("av").astype(BF16)
    khist[nxt] = kb
    vhist[nxt] = vb
    lane = lax.broadcasted_iota(jnp.int32, (1, LANES), 1)
    low_half = lane < ATT_HEAD_DIM
    ones_lane = (ATT_HEAD_DIM, 0)
    hist_neg = jnp.where(t == 0, NEG_INF, 0.0)
    one = jnp.ones((), BF16)
    v_ones = {}
    for p in range(ATT_HEADS // 2):
        ls = slice(p * LANES, (p + 1) * LANES)
        for hh, ol in enumerate(ones_lane):
            v_ones[p, hh] = (jnp.where(lane == ol, one, vhist[cur, :, ls]), jnp.where(lane == ol, one, vb[:, ls]))

    blocks = [(p, qb, hh) for p in range(ATT_HEADS // 2) for qb in range(TS // QB) for hh in range(2)]
    live = [{} for _ in blocks]

    def scores(n):
        p, qb, hh = blocks[n]
        ls = slice(p * LANES, (p + 1) * LANES)
        n_hist = HIST - qb * QB
        mine = low_half if hh == 0 else jnp.logical_not(low_half)
        qm = jnp.where(mine, aq[qb * QB:(qb + 1) * QB, ls], 0.0).astype(BF16)
        k_hist = khist[cur, qb * QB:HIST, ls]
        k_cur = kb[0:KW - n_hist, ls]
        live[n]["s"] = jnp.concatenate([_dot_nt(qm, k_hist) + hist_neg, _dot_nt(qm, k_cur)], axis=1)

    def row_max(n):
        p, _, hh = blocks[n]
        s = live[n].pop("s") + bias_ref[2 * p + hh]
        live[n]["s"] = s
        live[n]["m"] = jnp.max(s, axis=-1, keepdims=True)

    def exponent(n):
        live[n]["e"] = jnp.exp2(live[n].pop("s") - live[n].pop("m")).astype(BF16)

    def weighted_values(n):
        p, qb, hh = blocks[n]
        n_hist = HIST - qb * QB
        v_hist, v_cur = v_ones[p, hh][0][qb * QB:HIST], v_ones[p, hh][1][0:KW - n_hist]
        e = live[n].pop("e")
        live[n]["pv"] = _dot(e[:, :n_hist], v_hist) + _dot(e[:, n_hist:], v_cur)

    def normalise(n):
        hh = blocks[n][2]
        pv = live[n].pop("pv")
        live[n]["o"] = pv / pv[:, ones_lane[hh]:ones_lane[hh] + 1]

    stages = (scores, row_max, exponent, weighted_values, normalise)
    for step in range(len(blocks) + len(stages) - 1):
        for depth, stage in enumerate(stages):
            if 0 <= step - depth < len(blocks):
                stage(step - depth)

    o = {blk: live[n]["o"] for n, blk in enumerate(blocks)}
    return jnp.concatenate(
        [jnp.concatenate([jnp.where(low_half, o[p, qb, 0], o[p, qb, 1]) for qb in range(TS // QB)], axis=0)
         for p in range(ATT_HEADS // 2)], axis=1)


def _ffn_kernel(h_ref, nffn_ref, wup_ref, wdn_ref, nfin_ref, o_ref, *, final_norm):
    h = h_ref[...]
    xn = _rmsnorm(h, nffn_ref[...]).astype(BF16)
    acc = h
    for c in range(D_FF // FF_CHUNK):
        u = jnp.maximum(_dot(xn, wup_ref[:, c * FF_CHUNK:(c + 1) * FF_CHUNK]), 0.0)
        acc = acc + _dot((u * u).astype(BF16), wdn_ref[c * FF_CHUNK:(c + 1) * FF_CHUNK, :])
    if final_norm:
        acc = _rmsnorm(acc, nfin_ref[...])
    o_ref[...] = acc


def _const_spec(shape):
    nd = len(shape)
    return pl.BlockSpec(shape, lambda *_: (0,) * nd, pipeline_mode=pl.Buffered(1))


def _mixer_constants():
    i = np.arange(TS)
    later = ((i[None, :] > i[:, None]) & (i[None, :] // CHUNK == i[:, None] // CHUNK))
    v = np.arange(GLA_WIDTH)
    k = np.arange(KEY_PAD)
    hmask = ((v[:, None] // GLA_DV) == (k[None, :] // GLA_DK)) & (k[None, :] < GLA_KEY_WIDTH)
    hsum = (v[:, None] // GLA_DV) == (v[None, :] // GLA_DV)
    return (jnp.asarray(later, BF16), jnp.asarray(hmask, F32), jnp.asarray(hsum, BF16))


def _bias_table(rel_bias):
    i = np.arange(QB)[:, None]
    c = np.arange(KW)[None, :]
    band = (c // CHUNK >= i // CHUNK) & (c // CHUNK <= i // CHUNK + ATT_LEFT_CHUNKS)
    period = 1024
    u = np.arange(period)
    dist = np.where(u < KW, HIST - u, HIST + period - u)
    w_idx = np.clip(dist, -MAX_REL_DIST, MAX_REL_DIST) + MAX_REL_DIST
    assert KW + QB <= period and (w_idx[:385] == 256).all() and (w_idx[385:641] == np.arange(255, -1, -1)).all() \
        and (w_idx[641:768] == 0).all() and (w_idx[768:] == 256).all()
    rb = rel_bias.astype(F32)
    nh = rb.shape[0]
    w = jnp.concatenate([jnp.broadcast_to(rb[:, 256:257], (nh, 385)), rb[:, 255::-1],
                         jnp.broadcast_to(rb[:, 0:1], (nh, 127)), jnp.broadcast_to(rb[:, 256:257], (nh, 256))], axis=1)
    skew = jnp.tile(w, (1, QB))[:, :QB * (period - 1)].reshape(nh, QB, period - 1)[:, :, :KW]
    return jnp.where(band[None], skew * LOG2E, NEG_INF)


def _pad_last(w, width):
    return jnp.pad(w, [(0, 0)] * (w.ndim - 1) + [(0, width - w.shape[-1])])


def _pack_w_in(w_in):
    sizes = [s[1] for s in _SEGS]
    splits = np.cumsum(sizes)[:-1].tolist()
    parts = {s[0]: _pad_last(p, s[2]) for p, s in zip(jnp.split(w_in, splits, axis=-1), _SEGS)}
    return jnp.concatenate([parts[name] for name in _PACK_ORDER], axis=-1).astype(BF16)


def _layer_spec(shape, l):
    nd = len(shape) - 1
    return pl.BlockSpec((None,) + tuple(shape[1:]), lambda *_: (l,) + (0,) * nd, pipeline_mode=pl.Buffered(1))


def _mixer_layer(h, l, layered, shared):
    B, S, D = h.shape
    in_specs = [pl.BlockSpec((None, TS, D), lambda b, t: (b, t, 0))] + [_layer_spec(a.shape, l) for a in layered] \
        + [_const_spec(a.shape) for a in shared]
    return pl.pallas_call(
        _mixer_kernel,
        grid=(B, S // TS),
        in_specs=in_specs,
        out_specs=pl.BlockSpec((None, TS, D), lambda b, t: (b, t, 0)),
        out_shape=jax.ShapeDtypeStruct((B, S, D), F32),
        scratch_shapes=[pltpu.VMEM((2, HIST, ATT_WIDTH), BF16), pltpu.VMEM((2, HIST, ATT_WIDTH), BF16),
                        pltpu.VMEM((2, CONV_PAD, CONV_WIDTH), F32),
                        pltpu.VMEM((CONV_PAD + TS, CONV_WIDTH), F32),
                        pltpu.VMEM((SUBLANES - 1, SHIFT_ROWS, CONV_WIDTH), F32),
                        pltpu.VMEM((GLA_WIDTH, KEY_PAD), F32)],
        compiler_params=pltpu.CompilerParams(dimension_semantics=("arbitrary", "arbitrary"),
                                             vmem_limit_bytes=VMEM_LIMIT_BYTES),
        name="mixer",
    )(h, *layered, *shared)


def _ffn_layer(h2, l, layered, nfin, final_norm):
    T, D = h2.shape
    in_specs = [pl.BlockSpec((TM, D), lambda i: (i, 0))] + [_layer_spec(a.shape, l) for a in layered] \
        + [_const_spec(nfin.shape)]
    return pl.pallas_call(
        functools.partial(_ffn_kernel, final_norm=final_norm),
        grid=(T // TM,),
        in_specs=in_specs,
        out_specs=pl.BlockSpec((TM, D), lambda i: (i, 0)),
        out_shape=jax.ShapeDtypeStruct((T, D), F32),
        compiler_params=pltpu.CompilerParams(dimension_semantics=("arbitrary",),
                                             vmem_limit_bytes=VMEM_LIMIT_BYTES),
        name="ffn",
    )(h2, *layered, nfin)


def kernel(x, norm_mix, w_in, w_gla_gate, b_gla_gate, gla_norm, w_dw, b_dw, conv_ln_g, conv_ln_b, rel_bias, w_out,
           norm_ffn, w_up, w_down, norm_final):
    B, S, D = x.shape
    assert (D, S % (2 * TS), (B * S) % TM) == (D_MODEL, 0, 0)
    rows = lambda a: a[:, None, :].astype(F32)
    wgate = jnp.pad(w_gla_gate, ((0, 0), (0, LR_PAD - GLA_GATE_RANK), (0, KEY_PAD - GLA_KEY_WIDTH))).astype(BF16)
    bias = _bias_table(rel_bias.reshape(DEPTH * ATT_HEADS, -1)).reshape(DEPTH, ATT_HEADS, QB, KW)
    mixer_params = (rows(norm_mix), _pack_w_in(w_in), wgate, _pad_last(rows(b_gla_gate), KEY_PAD), rows(gla_norm),
                    w_dw.astype(F32), rows(b_dw), rows(conv_ln_g), rows(conv_ln_b), bias, w_out.astype(BF16))
    ffn_params = (rows(norm_ffn), w_up.astype(BF16), w_down.astype(BF16))
    shared = _mixer_constants()
    nfin = norm_final[None, :].astype(F32)
    h = x
    for l in range(DEPTH):
        h = _mixer_layer(h, l, mixer_params, shared)
        h = _ffn_layer(h.reshape(B * S, D), l, ffn_params, nfin, final_norm=(l == DEPTH - 1)).reshape(B, S, D)
    return h
```

```python
import functools

import numpy as np
import jax
import jax.numpy as jnp
from jax import lax
from jax.experimental import pallas as pl
from jax.experimental.pallas import tpu as pltpu

D_MODEL = 1024
DEPTH = 2
CHUNK = 64
GLA_DV, GLA_DK, GLA_HEADS = 96, 48, 4
GLA_WIDTH = GLA_HEADS * GLA_DV
GLA_KEY_WIDTH = GLA_HEADS * GLA_DK
GLA_GATE_RANK = 16
GLA_GATE_TAU = 16.0
CONV_WIDTH = 256
CONV_KERNEL = 31
ATT_HEAD_DIM = 64
ATT_HEADS = 6
ATT_WIDTH = ATT_HEADS * ATT_HEAD_DIM
ATT_LEFT_CHUNKS = 8
MAX_REL_DIST = 128
D_FF = 4 * D_MODEL
EPS = 1e-6
LOG2E = 1.4426950408889634
NEG_INF = -1e30

LANES = 128
SUBLANES = 8
VMEM_LIMIT_BYTES = 56 * 1024 * 1024

TS = 512
NCH = TS // CHUNK
HIST = ATT_LEFT_CHUNKS * CHUNK
QB = 256
KW = QB + HIST
BIAS_PERIOD = 1024
CONV_PAD = 32
SHIFT_ROWS = TS + CONV_PAD - SUBLANES
TM = 1024
FF_CHUNK = 1024

KEY_PAD = 256
LR_PAD = LANES
_SEGS = (("gq", GLA_KEY_WIDTH, KEY_PAD), ("gk", GLA_KEY_WIDTH, KEY_PAD), ("gv", GLA_WIDTH, GLA_WIDTH),
         ("gg", GLA_WIDTH, GLA_WIDTH), ("lr", GLA_GATE_RANK, LR_PAD), ("ca", CONV_WIDTH, CONV_WIDTH),
         ("cb", CONV_WIDTH, CONV_WIDTH), ("aq", ATT_WIDTH, ATT_WIDTH), ("ak", ATT_WIDTH, ATT_WIDTH),
         ("av", ATT_WIDTH, ATT_WIDTH))
_PACK_ORDER = ("gq", "gk", "gv", "lr", "gg", "aq", "ca", "cb", "ak", "av")
IN_GROUP = 1024
SEG = {}
_off = 0
for _name in _PACK_ORDER:
    _wp = {s[0]: s[2] for s in _SEGS}[_name]
    assert _off // IN_GROUP == (_off + _wp - 1) // IN_GROUP
    SEG[_name] = (_off, _off + _wp)
    _off += _wp
D_IN_PAD = _off
assert D_IN_PAD % IN_GROUP == 0

F32 = jnp.float32
BF16 = jnp.bfloat16


def _dot(a, b):
    return jnp.dot(a, b, preferred_element_type=F32)


def _dot_nt(a, b):
    return lax.dot_general(a, b, (((1,), (1,)), ((), ())), preferred_element_type=F32)


def _dot_tn(a, b):
    return lax.dot_general(a, b, (((0,), (0,)), ((), ())), preferred_element_type=F32)


def _split_bf16(x):
    hi = x.astype(BF16)
    lo = (x - hi.astype(F32)).astype(BF16)
    return hi, lo


def _rmsnorm(x, g):
    return x * lax.rsqrt(jnp.mean(x * x, axis=-1, keepdims=True) + EPS) * g


def _mixer_kernel(h_ref, nmix_ref, win_ref, wgate_ref, bgate_ref, gnorm_ref, wdw_ref, bdw_ref, lng_ref, lnb_ref,
                  brow_ref, wout_ref, later_ref, hmask_ref, hsum_ref, o_ref, bias_ref, khist, vhist, ctail, cbuf, shifted,
                  st_ref):
    t = pl.program_id(1)

    @pl.when((pl.program_id(0) == 0) & (t == 0))
    def _build_bias_table():
        col = lax.broadcasted_iota(jnp.int32, (SUBLANES, KW), 1)
        for hd in range(ATT_HEADS):
            for a in range(QB // SUBLANES):
                start = BIAS_PERIOD - a * SUBLANES
                first_key = (a * SUBLANES // CHUNK) * CHUNK
                in_band = (col >= first_key) & (col < first_key + (ATT_LEFT_CHUNKS + 1) * CHUNK)
                bias_ref[hd, a * SUBLANES:(a + 1) * SUBLANES, :] = jnp.where(
                    in_band, brow_ref[hd, :, start:start + KW], NEG_INF)
    cur = lax.rem(t, 2)
    nxt = 1 - cur

    @pl.when(t == 0)
    def _start_of_sequence():
        khist[0] = jnp.zeros((HIST, ATT_WIDTH), BF16)
        vhist[0] = jnp.zeros((HIST, ATT_WIDTH), BF16)
        ctail[0] = jnp.zeros((CONV_PAD, CONV_WIDTH), F32)
        st_ref[...] = jnp.zeros_like(st_ref)

    h = h_ref[...]
    xn = _rmsnorm(h, nmix_ref[...]).astype(BF16)
    groups = {}

    def project(g):
        if g not in groups:
            groups[g] = _dot(xn, win_ref[:, g * IN_GROUP:(g + 1) * IN_GROUP])

    def proj(name):
        lo, hi = SEG[name]
        project(lo // IN_GROUP)
        return groups[lo // IN_GROUP][:, lo % IN_GROUP:(hi - 1) % IN_GROUP + 1]

    later_groups = [functools.partial(project, g) for g in range(1, D_IN_PAD // IN_GROUP)]
    o_gla = _gla_mixer(proj, wgate_ref, bgate_ref, gnorm_ref, later_ref, hmask_ref, hsum_ref, st_ref,
                       lambda: later_groups.pop(0)() if later_groups else None)
    o_conv = _conv_mixer(proj, cur, nxt, wdw_ref, bdw_ref, lng_ref, lnb_ref, ctail, cbuf, shifted)
    o_att = _attention_mixer(proj, t, cur, nxt, bias_ref, khist, vhist)
    mixed = jnp.concatenate([o_gla, o_conv, o_att], axis=1).astype(BF16)
    o_ref[...] = h + _dot(mixed, wout_ref[...])


def _gla_mixer(proj, wgate_ref, bgate_ref, gnorm_ref, later_ref, hmask_ref, hsum_ref, st_ref, between_links):
    gq = (proj("gq") * (GLA_DK ** -0.5)).astype(BF16)
    gk = proj("gk")
    gv = proj("gv").astype(BF16)
    z = _dot(proj("lr").astype(BF16), wgate_ref[...]) + bgate_ref[...]
    between_links()
    log_a = (jnp.minimum(z, 0.0) - jnp.log1p(jnp.exp(-jnp.abs(z)))) / GLA_GATE_TAU
    la_hi, la_lo = _split_bf16(log_a)
    later = later_ref[...]
    rev = _dot(later, la_hi) + _dot(later, la_lo)
    between_links()
    kdec = (gk * jnp.exp(rev)).astype(BF16)
    chunk_log = rev + log_a
    hmask = hmask_ref[...]
    chunks = [slice(c * CHUNK, (c + 1) * CHUNK) for c in range(NCH)]
    kv_t = [_dot_tn(gv[rows], kdec[rows]) * hmask for rows in chunks]
    decay = [jnp.exp(chunk_log[rows.start:rows.start + 1, :]) for rows in chunks]
    st = st_ref[...]
    states = []
    for c in range(NCH):
        st = st * decay[c] + kv_t[c]
        states.append(st.astype(BF16))
    st_ref[...] = st
    o_chunks = [_dot_nt(gq[rows], states[c]) for c, rows in enumerate(chunks)]
    og = jnp.concatenate(o_chunks, axis=0)
    sq_hi, sq_lo = _split_bf16(og * og)
    hsum = hsum_ref[...]
    ms = (_dot(sq_hi, hsum) + _dot(sq_lo, hsum)) / GLA_DV
    gg = proj("gg")
    return og * lax.rsqrt(ms + EPS) * gnorm_ref[...] * (gg * jax.nn.sigmoid(gg))


def _conv_mixer(proj, cur, nxt, wdw_ref, bdw_ref, lng_ref, lnb_ref, ctail, cbuf, shifted):
    hg = proj("ca") * jax.nn.sigmoid(proj("cb"))
    cbuf[0:CONV_PAD, :] = ctail[cur]
    cbuf[CONV_PAD:CONV_PAD + TS, :] = hg
    ctail[nxt] = hg[TS - CONV_PAD:TS, :]
    for s in range(1, SUBLANES):
        shifted[s - 1] = cbuf[pl.ds(s, SHIFT_ROWS), :]
    base = CONV_PAD - (CONV_KERNEL - 1)
    rows = []
    for rb in range(TS // CHUNK):
        acc = jnp.zeros((CHUNK, CONV_WIDTH), F32)
        for j in range(CONV_KERNEL):
            a, s = divmod(base + j, SUBLANES)
            src = cbuf if s == 0 else shifted.at[s - 1]
            acc = acc + src[pl.ds(rb * CHUNK + a * SUBLANES, CHUNK), :] * wdw_ref[j:j + 1, :]
        rows.append(acc)
    hc = jnp.concatenate(rows, axis=0) + bdw_ref[...]
    mu = jnp.mean(hc, axis=-1, keepdims=True)
    var = jnp.mean(jnp.square(hc - mu), axis=-1, keepdims=True)
    hc = (hc - mu) * lax.rsqrt(var + EPS) * lng_ref[...] + lnb_ref[...]
    return hc * jax.nn.sigmoid(hc)


def _attention_mixer(proj, t, cur, nxt, bias_ref, khist, vhist):
    aq = proj("aq") * (ATT_HEAD_DIM ** -0.5 * LOG2E)
    kb = proj("ak").astype(BF16)
    vb = proj("av").astype(BF16)
    khist[nxt] = kb
    vhist[nxt] = vb
    lane = lax.broadcasted_iota(jnp.int32, (1, LANES), 1)
    low_half = lane < ATT_HEAD_DIM
    ones_lane = (ATT_HEAD_DIM, 0)
    hist_neg = jnp.where(t == 0, NEG_INF, 0.0)
    one = jnp.ones((), BF16)
    v_ones = {}
    for p in range(ATT_HEADS // 2):
        ls = slice(p * LANES, (p + 1) * LANES)
        for hh, ol in enumerate(ones_lane):
            v_ones[p, hh] = (jnp.where(lane == ol, one, vhist[cur, :, ls]), jnp.where(lane == ol, one, vb[:, ls]))

    blocks = [(p, qb, hh) for p in range(ATT_HEADS // 2) for qb in range(TS // QB) for hh in range(2)]
    live = [{} for _ in blocks]

    def scores(n):
        p, qb, hh = blocks[n]
        ls = slice(p * LANES, (p + 1) * LANES)
        n_hist = HIST - qb * QB
        mine = low_half if hh == 0 else jnp.logical_not(low_half)
        qm = jnp.where(mine, aq[qb * QB:(qb + 1) * QB, ls], 0.0).astype(BF16)
        k_hist = khist[cur, qb * QB:HIST, ls]
        k_cur = kb[0:KW - n_hist, ls]
        live[n]["s"] = jnp.concatenate([_dot_nt(qm, k_hist) + hist_neg, _dot_nt(qm, k_cur)], axis=1)

    def row_max(n):
        p, _, hh = blocks[n]
        s = live[n].pop("s") + bias_ref[2 * p + hh]
        live[n]["s"] = s
        live[n]["m"] = jnp.max(s, axis=-1, keepdims=True)

    def exponent(n):
        live[n]["e"] = jnp.exp2(live[n].pop("s") - live[n].pop("m")).astype(BF16)

    def weighted_values(n):
        p, qb, hh = blocks[n]
        n_hist = HIST - qb * QB
        v_hist, v_cur = v_ones[p, hh][0][qb * QB:HIST], v_ones[p, hh][1][0:KW - n_hist]
        e = live[n].pop("e")
        live[n]["pv"] = _dot(e[:, :n_hist], v_hist) + _dot(e[:, n_hist:], v_cur)

    def normalise(n):
        hh = blocks[n][2]
        pv = live[n].pop("pv")
        live[n]["o"] = pv / pv[:, ones_lane[hh]:ones_lane[hh] + 1]

    stages = (scores, row_max, exponent, weighted_values, normalise)
    for step in range(len(blocks) + len(stages) - 1):
        for depth, stage in enumerate(stages):
            if 0 <= step - depth < len(blocks):
                stage(step - depth)

    o = {blk: live[n]["o"] for n, blk in enumerate(blocks)}
    return jnp.concatenate(
        [jnp.concatenate([jnp.where(low_half, o[p, qb, 0], o[p, qb, 1]) for qb in range(TS // QB)], axis=0)
         for p in range(ATT_HEADS // 2)], axis=1)


def _ffn_kernel(h_ref, nffn_ref, wup_ref, wdn_ref, nfin_ref, o_ref, *, final_norm):
    h = h_ref[...]
    xn = _rmsnorm(h, nffn_ref[...]).astype(BF16)
    acc = h
    for c in range(D_FF // FF_CHUNK):
        u = jnp.maximum(_dot(xn, wup_ref[:, c * FF_CHUNK:(c + 1) * FF_CHUNK]), 0.0)
        acc = acc + _dot((u * u).astype(BF16), wdn_ref[c * FF_CHUNK:(c + 1) * FF_CHUNK, :])
    if final_norm:
        acc = _rmsnorm(acc, nfin_ref[...])
    o_ref[...] = acc


def _const_spec(shape):
    nd = len(shape)
    return pl.BlockSpec(shape, lambda *_: (0,) * nd, pipeline_mode=pl.Buffered(1))


def _mixer_constants():
    i = np.arange(TS)
    later = ((i[None, :] > i[:, None]) & (i[None, :] // CHUNK == i[:, None] // CHUNK))
    v = np.arange(GLA_WIDTH)
    k = np.arange(KEY_PAD)
    hmask = ((v[:, None] // GLA_DV) == (k[None, :] // GLA_DK)) & (k[None, :] < GLA_KEY_WIDTH)
    hsum = (v[:, None] // GLA_DV) == (v[None, :] // GLA_DV)
    return (jnp.asarray(later, BF16), jnp.asarray(hmask, F32), jnp.asarray(hsum, BF16))


def _bias_rows(rel_bias):
    u = np.arange(BIAS_PERIOD)
    dist = np.where(u < KW, HIST - u, HIST + BIAS_PERIOD - u)
    w_idx = np.clip(dist, -MAX_REL_DIST, MAX_REL_DIST) + MAX_REL_DIST
    assert KW + QB <= BIAS_PERIOD and (w_idx[:385] == 256).all() and (w_idx[385:641] == np.arange(255, -1, -1)).all() \
        and (w_idx[641:768] == 0).all() and (w_idx[768:] == 256).all()
    rb = rel_bias.astype(F32) * LOG2E
    lead = rb.shape[:-1]
    w = jnp.concatenate([jnp.broadcast_to(rb[..., 256:257], lead + (385,)), rb[..., 255::-1],
                         jnp.broadcast_to(rb[..., 0:1], lead + (127,)),
                         jnp.broadcast_to(rb[..., 256:257], lead + (256,))], axis=-1)
    advanced = jnp.stack([jnp.roll(w, b, axis=-1) for b in range(SUBLANES)], axis=-2)
    return jnp.concatenate([advanced, advanced], axis=-1)


def _pad_last(w, width):
    return jnp.pad(w, [(0, 0)] * (w.ndim - 1) + [(0, width - w.shape[-1])])


def _pack_w_in(w_in):
    sizes = [s[1] for s in _SEGS]
    splits = np.cumsum(sizes)[:-1].tolist()
    parts = {s[0]: _pad_last(p, s[2]) for p, s in zip(jnp.split(w_in, splits, axis=-1), _SEGS)}
    return jnp.concatenate([parts[name] for name in _PACK_ORDER], axis=-1).astype(BF16)


def _layer_spec(shape, l):
    nd = len(shape) - 1
    return pl.BlockSpec((None,) + tuple(shape[1:]), lambda *_: (l,) + (0,) * nd, pipeline_mode=pl.Buffered(1))


def _mixer_layer(h, l, layered, shared):
    B, S, D = h.shape
    in_specs = [pl.BlockSpec((None, TS, D), lambda b, t: (b, t, 0))] + [_layer_spec(a.shape, l) for a in layered] \
        + [_const_spec(a.shape) for a in shared]
    return pl.pallas_call(
        _mixer_kernel,
        grid=(B, S // TS),
        in_specs=in_specs,
        out_specs=pl.BlockSpec((None, TS, D), lambda b, t: (b, t, 0)),
        out_shape=jax.ShapeDtypeStruct((B, S, D), F32),
        scratch_shapes=[pltpu.VMEM((ATT_HEADS, QB, KW), F32),
                        pltpu.VMEM((2, HIST, ATT_WIDTH), BF16), pltpu.VMEM((2, HIST, ATT_WIDTH), BF16),
                        pltpu.VMEM((2, CONV_PAD, CONV_WIDTH), F32),
                        pltpu.VMEM((CONV_PAD + TS, CONV_WIDTH), F32),
                        pltpu.VMEM((SUBLANES - 1, SHIFT_ROWS, CONV_WIDTH), F32),
                        pltpu.VMEM((GLA_WIDTH, KEY_PAD), F32)],
        compiler_params=pltpu.CompilerParams(dimension_semantics=("arbitrary", "arbitrary"),
                                             vmem_limit_bytes=VMEM_LIMIT_BYTES),
        name="mixer",
    )(h, *layered, *shared)


def _ffn_layer(h2, l, layered, nfin, final_norm):
    T, D = h2.shape
    in_specs = [pl.BlockSpec((TM, D), lambda i: (i, 0))] + [_layer_spec(a.shape, l) for a in layered] \
        + [_const_spec(nfin.shape)]
    return pl.pallas_call(
        functools.partial(_ffn_kernel, final_norm=final_norm),
        grid=(T // TM,),
        in_specs=in_specs,
        out_specs=pl.BlockSpec((TM, D), lambda i: (i, 0)),
        out_shape=jax.ShapeDtypeStruct((T, D), F32),
        compiler_params=pltpu.CompilerParams(dimension_semantics=("arbitrary",),
                                             vmem_limit_bytes=VMEM_LIMIT_BYTES),
        name="ffn",
    )(h2, *layered, nfin)


def kernel(x, norm_mix, w_in, w_gla_gate, b_gla_gate, gla_norm, w_dw, b_dw, conv_ln_g, conv_ln_b, rel_bias, w_out,
           norm_ffn, w_up, w_down, norm_final):
    B, S, D = x.shape
    assert (D, S % (2 * TS), (B * S) % TM) == (D_MODEL, 0, 0)
    rows = lambda a: a[:, None, :].astype(F32)
    wgate = jnp.pad(w_gla_gate, ((0, 0), (0, LR_PAD - GLA_GATE_RANK), (0, KEY_PAD - GLA_KEY_WIDTH))).astype(BF16)
    mixer_params = (rows(norm_mix), _pack_w_in(w_in), wgate, _pad_last(rows(b_gla_gate), KEY_PAD), rows(gla_norm),
                    w_dw.astype(F32), rows(b_dw), rows(conv_ln_g), rows(conv_ln_b), _bias_rows(rel_bias),
                    w_out.astype(BF16))
    ffn_params = (rows(norm_ffn), w_up.astype(BF16), w_down.astype(BF16))
    shared = _mixer_constants()
    nfin = norm_final[None, :].astype(F32)
    h = x
    for l in range(DEPTH):
        h = _mixer_layer(h, l, mixer_params, shared)
        h = _ffn_layer(h.reshape(B * S, D), l, ffn_params, nfin, final_norm=(l == DEPTH - 1)).reshape(B, S, D)
    return h
```

```python
import functools

import numpy as np
import jax
import jax.numpy as jnp
from jax import lax
from jax.experimental import pallas as pl
from jax.experimental.pallas import tpu as pltpu

D_MODEL = 1024
DEPTH = 2
CHUNK = 64
GLA_DV, GLA_DK, GLA_HEADS = 96, 48, 4
GLA_WIDTH = GLA_HEADS * GLA_DV
GLA_KEY_WIDTH = GLA_HEADS * GLA_DK
GLA_GATE_RANK = 16
GLA_GATE_TAU = 16.0
CONV_WIDTH = 256
CONV_KERNEL = 31
ATT_HEAD_DIM = 64
ATT_HEADS = 6
ATT_WIDTH = ATT_HEADS * ATT_HEAD_DIM
ATT_LEFT_CHUNKS = 8
MAX_REL_DIST = 128
D_FF = 4 * D_MODEL
EPS = 1e-6
LOG2E = 1.4426950408889634
NEG_INF = -1e30

LANES = 128
SUBLANES = 8
VMEM_LIMIT_BYTES = 56 * 1024 * 1024

TS = 512
NCH = TS // CHUNK
HIST = ATT_LEFT_CHUNKS * CHUNK
QB = 256
KW = QB + HIST
BIAS_PERIOD = 1024
CONV_PAD = 32
SHIFT_ROWS = TS + CONV_PAD - SUBLANES
TM = 1024
FF_CHUNK = 1024

KEY_PAD = 256
LR_PAD = LANES
_SEGS = (("gq", GLA_KEY_WIDTH, KEY_PAD), ("gk", GLA_KEY_WIDTH, KEY_PAD), ("gv", GLA_WIDTH, GLA_WIDTH),
         ("gg", GLA_WIDTH, GLA_WIDTH), ("lr", GLA_GATE_RANK, LR_PAD), ("ca", CONV_WIDTH, CONV_WIDTH),
         ("cb", CONV_WIDTH, CONV_WIDTH), ("aq", ATT_WIDTH, ATT_WIDTH), ("ak", ATT_WIDTH, ATT_WIDTH),
         ("av", ATT_WIDTH, ATT_WIDTH))
_PACK_ORDER = ("gq", "gk", "gv", "lr", "gg", "aq", "ca", "cb", "ak", "av")
IN_GROUP = 1024
SEG = {}
_off = 0
for _name in _PACK_ORDER:
    _wp = {s[0]: s[2] for s in _SEGS}[_name]
    assert _off // IN_GROUP == (_off + _wp - 1) // IN_GROUP
    SEG[_name] = (_off, _off + _wp)
    _off += _wp
D_IN_PAD = _off
assert D_IN_PAD % IN_GROUP == 0

F32 = jnp.float32
BF16 = jnp.bfloat16


def _dot(a, b):
    return jnp.dot(a, b, preferred_element_type=F32)


def _dot_nt(a, b):
    return lax.dot_general(a, b, (((1,), (1,)), ((), ())), preferred_element_type=F32)


def _dot_tn(a, b):
    return lax.dot_general(a, b, (((0,), (0,)), ((), ())), preferred_element_type=F32)


def _split_bf16(x):
    hi = x.astype(BF16)
    lo = (x - hi.astype(F32)).astype(BF16)
    return hi, lo


def _rmsnorm(x, g):
    return x * lax.rsqrt(jnp.mean(x * x, axis=-1, keepdims=True) + EPS) * g


def _mixer_kernel(h_ref, nmix_ref, win_ref, wgate_ref, bgate_ref, gnorm_ref, wdw_ref, bdw_ref, lng_ref, lnb_ref,
                  brow_ref, wout_ref, later_ref, hmask_ref, hsum_ref, o_ref, bias_ref, khist, vhist, ctail, cbuf, shifted,
                  st_ref):
    t = pl.program_id(1)

    @pl.when((pl.program_id(0) == 0) & (t == 0))
    def _build_bias_table():
        col = lax.broadcasted_iota(jnp.int32, (SUBLANES, KW), 1)
        for hd in range(ATT_HEADS):
            for a in range(QB // SUBLANES):
                start = BIAS_PERIOD - a * SUBLANES
                first_key = (a * SUBLANES // CHUNK) * CHUNK
                in_band = (col >= first_key) & (col < first_key + (ATT_LEFT_CHUNKS + 1) * CHUNK)
                bias_ref[hd, a * SUBLANES:(a + 1) * SUBLANES, :] = jnp.where(
                    in_band, brow_ref[hd, :, start:start + KW], NEG_INF)
    cur = lax.rem(t, 2)
    nxt = 1 - cur

    @pl.when(t == 0)
    def _start_of_sequence():
        khist[0] = jnp.zeros((HIST, ATT_WIDTH), BF16)
        vhist[0] = jnp.zeros((HIST, ATT_WIDTH), BF16)
        ctail[0] = jnp.zeros((CONV_PAD, CONV_WIDTH), F32)
        st_ref[...] = jnp.zeros_like(st_ref)

    h = h_ref[...]
    xn = _rmsnorm(h, nmix_ref[...]).astype(BF16)
    groups = {}

    def project(g):
        if g not in groups:
            groups[g] = _dot(xn, win_ref[:, g * IN_GROUP:(g + 1) * IN_GROUP])

    def proj(name):
        lo, hi = SEG[name]
        project(lo // IN_GROUP)
        return groups[lo // IN_GROUP][:, lo % IN_GROUP:(hi - 1) % IN_GROUP + 1]

    later_groups = [functools.partial(project, g) for g in range(1, D_IN_PAD // IN_GROUP)]
    o_gla = _gla_mixer(proj, wgate_ref, bgate_ref, gnorm_ref, later_ref, hmask_ref, hsum_ref, st_ref,
                       lambda: later_groups.pop(0)() if later_groups else None)
    o_conv = _conv_mixer(proj, cur, nxt, wdw_ref, bdw_ref, lng_ref, lnb_ref, ctail, cbuf, shifted)
    o_att = _attention_mixer(proj, t, cur, nxt, bias_ref, khist, vhist)
    mixed = jnp.concatenate([o_gla, o_conv, o_att], axis=1).astype(BF16)
    o_ref[...] = h + _dot(mixed, wout_ref[...])


def _gla_mixer(proj, wgate_ref, bgate_ref, gnorm_ref, later_ref, hmask_ref, hsum_ref, st_ref, between_links):
    gq = (proj("gq") * (GLA_DK ** -0.5)).astype(BF16)
    gk = proj("gk")
    gv = proj("gv").astype(BF16)
    z = _dot(proj("lr").astype(BF16), wgate_ref[...]) + bgate_ref[...]
    between_links()
    log_a = (jnp.minimum(z, 0.0) - jnp.log1p(jnp.exp(-jnp.abs(z)))) / GLA_GATE_TAU
    la_hi, la_lo = _split_bf16(log_a)
    later = later_ref[...]
    rev = _dot(later, la_hi) + _dot(later, la_lo)
    between_links()
    kdec = (gk * jnp.exp(rev)).astype(BF16)
    chunk_log = rev + log_a
    hmask = hmask_ref[...]
    chunks = [slice(c * CHUNK, (c + 1) * CHUNK) for c in range(NCH)]
    kv_t = [_dot_tn(gv[rows], kdec[rows]) * hmask for rows in chunks]
    decay = [jnp.exp(chunk_log[rows.start:rows.start + 1, :]) for rows in chunks]
    st = st_ref[...]
    states = []
    for c in range(NCH):
        st = st * decay[c] + kv_t[c]
        states.append(st.astype(BF16))
    st_ref[...] = st
    o_chunks = [_dot_nt(gq[rows], states[c]) for c, rows in enumerate(chunks)]
    og = jnp.concatenate(o_chunks, axis=0)
    sq_hi, sq_lo = _split_bf16(og * og)
    hsum = hsum_ref[...]
    ms = (_dot(sq_hi, hsum) + _dot(sq_lo, hsum)) / GLA_DV
    gg = proj("gg")
    return og * lax.rsqrt(ms + EPS) * gnorm_ref[...] * (gg * jax.nn.sigmoid(gg))


def _conv_mixer(proj, cur, nxt, wdw_ref, bdw_ref, lng_ref, lnb_ref, ctail, cbuf, shifted):
    hg = proj("ca") * jax.nn.sigmoid(proj("cb"))
    cbuf[0:CONV_PAD, :] = ctail[cur]
    cbuf[CONV_PAD:CONV_PAD + TS, :] = hg
    ctail[nxt] = hg[TS - CONV_PAD:TS, :]
    for s in range(1, SUBLANES):
        shifted[s - 1] = cbuf[pl.ds(s, SHIFT_ROWS), :]
    base = CONV_PAD - (CONV_KERNEL - 1)
    rows = []
    for rb in range(TS // CHUNK):
        acc = jnp.zeros((CHUNK, CONV_WIDTH), F32)
        for j in range(CONV_KERNEL):
            a, s = divmod(base + j, SUBLANES)
            src = cbuf if s == 0 else shifted.at[s - 1]
            acc = acc + src[pl.ds(rb * CHUNK + a * SUBLANES, CHUNK), :] * wdw_ref[j:j + 1, :]
        rows.append(acc)
    hc = jnp.concatenate(rows, axis=0) + bdw_ref[...]
    mu = jnp.mean(hc, axis=-1, keepdims=True)
    var = jnp.mean(jnp.square(hc - mu), axis=-1, keepdims=True)
    hc = (hc - mu) * lax.rsqrt(var + EPS) * lng_ref[...] + lnb_ref[...]
    return hc * jax.nn.sigmoid(hc)


def _attention_mixer(proj, t, cur, nxt, bias_ref, khist, vhist):
    aq = proj("aq") * (ATT_HEAD_DIM ** -0.5 * LOG2E)
    kb = proj("ak").astype(BF16)
    vb = proj("av").astype(BF16)
    khist[nxt] = kb
    vhist[nxt] = vb
    lane = lax.broadcasted_iota(jnp.int32, (1, LANES), 1)
    low_half = lane < ATT_HEAD_DIM
    ones_lane = (ATT_HEAD_DIM, 0)
    hist_neg = jnp.where(t == 0, NEG_INF, 0.0)
    one = jnp.ones((), BF16)
    v_ones = {}
    for p in range(ATT_HEADS // 2):
        ls = slice(p * LANES, (p + 1) * LANES)
        for hh, ol in enumerate(ones_lane):
            v_ones[p, hh] = (jnp.where(lane == ol, one, vhist[cur, :, ls]), jnp.where(lane == ol, one, vb[:, ls]))

    blocks = [(p, qb, hh) for p in range(ATT_HEADS // 2) for qb in range(TS // QB) for hh in range(2)]
    live = [{} for _ in blocks]

    def scores(n):
        p, qb, hh = blocks[n]
        ls = slice(p * LANES, (p + 1) * LANES)
        n_hist = HIST - qb * QB
        mine = low_half if hh == 0 else jnp.logical_not(low_half)
        qm = jnp.where(mine, aq[qb * QB:(qb + 1) * QB, ls], 0.0).astype(BF16)
        k_hist = khist[cur, qb * QB:HIST, ls]
        k_cur = kb[0:KW - n_hist, ls]
        live[n]["s"] = jnp.concatenate([_dot_nt(qm, k_hist) + hist_neg, _dot_nt(qm, k_cur)], axis=1)

    def row_max(n):
        p, _, hh = blocks[n]
        s = live[n].pop("s") + bias_ref[2 * p + hh]
        live[n]["s"] = s
        live[n]["m"] = jnp.max(s, axis=-1, keepdims=True)

    def exponent(n):
        live[n]["e"] = jnp.exp2(live[n].pop("s") - live[n].pop("m")).astype(BF16)

    def weighted_values(n):
        p, qb, hh = blocks[n]
        n_hist = HIST - qb * QB
        v_hist, v_cur = v_ones[p, hh][0][qb * QB:HIST], v_ones[p, hh][1][0:KW - n_hist]
        e = live[n].pop("e")
        live[n]["pv"] = _dot(e[:, :n_hist], v_hist) + _dot(e[:, n_hist:], v_cur)

    def normalise(n):
        hh = blocks[n][2]
        pv = live[n].pop("pv")
        live[n]["o"] = pv / pv[:, ones_lane[hh]:ones_lane[hh] + 1]

    stages = (scores, row_max, exponent, weighted_values, normalise)
    for step in range(len(blocks) + len(stages) - 1):
        for depth, stage in enumerate(stages):
            if 0 <= step - depth < len(blocks):
                stage(step - depth)

    o = {blk: live[n]["o"] for n, blk in enumerate(blocks)}
    return jnp.concatenate(
        [jnp.concatenate([jnp.where(low_half, o[p, qb, 0], o[p, qb, 1]) for qb in range(TS // QB)], axis=0)
         for p in range(ATT_HEADS // 2)], axis=1)


def _ffn_kernel(h_ref, nffn_ref, wup_ref, wdn_ref, nfin_ref, o_ref, *, final_norm):
    h = h_ref[...]
    xn = _rmsnorm(h, nffn_ref[...]).astype(BF16)
    acc = h
    for c in range(D_FF // FF_CHUNK):
        u = jnp.maximum(_dot(xn, wup_ref[:, c * FF_CHUNK:(c + 1) * FF_CHUNK]), 0.0)
        acc = acc + _dot((u * u).astype(BF16), wdn_ref[c * FF_CHUNK:(c + 1) * FF_CHUNK, :])
    if final_norm:
        acc = _rmsnorm(acc, nfin_ref[...])
    o_ref[...] = acc


def _const_spec(shape):
    nd = len(shape)
    return pl.BlockSpec(shape, lambda *_: (0,) * nd, pipeline_mode=pl.Buffered(1))


def _mixer_constants():
    i = np.arange(TS)
    later = ((i[None, :] > i[:, None]) & (i[None, :] // CHUNK == i[:, None] // CHUNK))
    v = np.arange(GLA_WIDTH)
    k = np.arange(KEY_PAD)
    hmask = ((v[:, None] // GLA_DV) == (k[None, :] // GLA_DK)) & (k[None, :] < GLA_KEY_WIDTH)
    hsum = (v[:, None] // GLA_DV) == (v[None, :] // GLA_DV)
    return (jnp.asarray(later, BF16), jnp.asarray(hmask, F32), jnp.asarray(hsum, BF16))


def _bias_rows(rel_bias):
    u = np.arange(BIAS_PERIOD)
    dist = np.where(u < KW, HIST - u, HIST + BIAS_PERIOD - u)
    w_idx = np.clip(dist, -MAX_REL_DIST, MAX_REL_DIST) + MAX_REL_DIST
    n_rel = 2 * MAX_REL_DIST + 1
    n_far = HIST - MAX_REL_DIST + 1
    n_near = KW - n_far - (n_rel - 1)
    pieces = [np.full(n_far, n_rel - 1), np.arange(n_rel - 2, -1, -1), np.zeros(n_near, int),
              np.full(BIAS_PERIOD - KW, n_rel - 1)]
    assert KW + QB <= BIAS_PERIOD and (w_idx == np.concatenate(pieces)).all()
    rb = rel_bias.astype(F32) * LOG2E
    lead = rb.shape[:-1]
    far, near = rb[..., n_rel - 1:n_rel], rb[..., 0:1]
    w = jnp.concatenate([jnp.broadcast_to(far, lead + (n_far,)), rb[..., n_rel - 2::-1],
                         jnp.broadcast_to(near, lead + (n_near,)),
                         jnp.broadcast_to(far, lead + (BIAS_PERIOD - KW,))], axis=-1)
    advanced = jnp.stack([jnp.roll(w, b, axis=-1) for b in range(SUBLANES)], axis=-2)
    return jnp.concatenate([advanced, advanced], axis=-1)


def _pad_last(w, width):
    return jnp.pad(w, [(0, 0)] * (w.ndim - 1) + [(0, width - w.shape[-1])])


def _pack_w_in(w_in):
    sizes = [s[1] for s in _SEGS]
    splits = np.cumsum(sizes)[:-1].tolist()
    parts = {s[0]: _pad_last(p, s[2]) for p, s in zip(jnp.split(w_in, splits, axis=-1), _SEGS)}
    return jnp.concatenate([parts[name] for name in _PACK_ORDER], axis=-1).astype(BF16)


def _layer_spec(shape, l):
    nd = len(shape) - 1
    return pl.BlockSpec((None,) + tuple(shape[1:]), lambda *_: (l,) + (0,) * nd, pipeline_mode=pl.Buffered(1))


def _mixer_layer(h, l, layered, shared):
    B, S, D = h.shape
    in_specs = [pl.BlockSpec((None, TS, D), lambda b, t: (b, t, 0))] + [_layer_spec(a.shape, l) for a in layered] \
        + [_const_spec(a.shape) for a in shared]
    return pl.pallas_call(
        _mixer_kernel,
        grid=(B, S // TS),
        in_specs=in_specs,
        out_specs=pl.BlockSpec((None, TS, D), lambda b, t: (b, t, 0)),
        out_shape=jax.ShapeDtypeStruct((B, S, D), F32),
        scratch_shapes=[pltpu.VMEM((ATT_HEADS, QB, KW), F32),
                        pltpu.VMEM((2, HIST, ATT_WIDTH), BF16), pltpu.VMEM((2, HIST, ATT_WIDTH), BF16),
                        pltpu.VMEM((2, CONV_PAD, CONV_WIDTH), F32),
                        pltpu.VMEM((CONV_PAD + TS, CONV_WIDTH), F32),
                        pltpu.VMEM((SUBLANES - 1, SHIFT_ROWS, CONV_WIDTH), F32),
                        pltpu.VMEM((GLA_WIDTH, KEY_PAD), F32)],
        compiler_params=pltpu.CompilerParams(dimension_semantics=("arbitrary", "arbitrary"),
                                             vmem_limit_bytes=VMEM_LIMIT_BYTES),
        name="mixer",
    )(h, *layered, *shared)


def _ffn_layer(h2, l, layered, nfin, final_norm):
    T, D = h2.shape
    in_specs = [pl.BlockSpec((TM, D), lambda i: (i, 0))] + [_layer_spec(a.shape, l) for a in layered] \
        + [_const_spec(nfin.shape)]
    return pl.pallas_call(
        functools.partial(_ffn_kernel, final_norm=final_norm),
        grid=(T // TM,),
        in_specs=in_specs,
        out_specs=pl.BlockSpec((TM, D), lambda i: (i, 0)),
        out_shape=jax.ShapeDtypeStruct((T, D), F32),
        compiler_params=pltpu.CompilerParams(dimension_semantics=("arbitrary",),
                                             vmem_limit_bytes=VMEM_LIMIT_BYTES),
        name="ffn",
    )(h2, *layered, nfin)


def kernel(x, norm_mix, w_in, w_gla_gate, b_gla_gate, gla_norm, w_dw, b_dw, conv_ln_g, conv_ln_b, rel_bias, w_out,
           norm_ffn, w_up, w_down, norm_final):
    B, S, D = x.shape
    assert (D, S % (2 * TS), (B * S) % TM) == (D_MODEL, 0, 0)
    rows = lambda a: a[:, None, :].astype(F32)
    wgate = jnp.pad(w_gla_gate, ((0, 0), (0, LR_PAD - GLA_GATE_RANK), (0, KEY_PAD - GLA_KEY_WIDTH))).astype(BF16)
    mixer_params = (rows(norm_mix), _pack_w_in(w_in), wgate, _pad_last(rows(b_gla_gate), KEY_PAD), rows(gla_norm),
                    w_dw.astype(F32), rows(b_dw), rows(conv_ln_g), rows(conv_ln_b), _bias_rows(rel_bias),
                    w_out.astype(BF16))
    ffn_params = (rows(norm_ffn), w_up.astype(BF16), w_down.astype(BF16))
    shared = _mixer_constants()
    nfin = norm_final[None, :].astype(F32)
    h = x
    for l in range(DEPTH):
        h = _mixer_layer(h, l, mixer_params, shared)
        h = _ffn_layer(h.reshape(B * S, D), l, ffn_params, nfin, final_norm=(l == DEPTH - 1)).reshape(B, S, D)
    return h
```

```python
import functools

import numpy as np
import jax
import jax.numpy as jnp
from jax import lax
from jax.experimental import pallas as pl
from jax.experimental.pallas import tpu as pltpu

D_MODEL = 1024
DEPTH = 2
CHUNK = 64
GLA_DV, GLA_DK, GLA_HEADS = 96, 48, 4
GLA_WIDTH = GLA_HEADS * GLA_DV
GLA_KEY_WIDTH = GLA_HEADS * GLA_DK
GLA_GATE_RANK = 16
GLA_GATE_TAU = 16.0
CONV_WIDTH = 256
CONV_KERNEL = 31
ATT_HEAD_DIM = 64
ATT_HEADS = 6
ATT_WIDTH = ATT_HEADS * ATT_HEAD_DIM
ATT_LEFT_CHUNKS = 8
MAX_REL_DIST = 128
D_FF = 4 * D_MODEL
EPS = 1e-6
LOG2E = 1.4426950408889634
NEG_INF = -1e30

LANES = 128
SUBLANES = 8
VMEM_LIMIT_BYTES = 56 * 1024 * 1024

TS = 512
NCH = TS // CHUNK
HIST = ATT_LEFT_CHUNKS * CHUNK
QB = 128
KW = QB + HIST
BIAS_PERIOD = 1024
CONV_PAD = 32
SHIFT_ROWS = TS + CONV_PAD - SUBLANES
TM = 1024
FF_CHUNK = 1024

KEY_PAD = 256
LR_PAD = LANES
_SEGS = (("gq", GLA_KEY_WIDTH, KEY_PAD), ("gk", GLA_KEY_WIDTH, KEY_PAD), ("gv", GLA_WIDTH, GLA_WIDTH),
         ("gg", GLA_WIDTH, GLA_WIDTH), ("lr", GLA_GATE_RANK, LR_PAD), ("ca", CONV_WIDTH, CONV_WIDTH),
         ("cb", CONV_WIDTH, CONV_WIDTH), ("aq", ATT_WIDTH, ATT_WIDTH), ("ak", ATT_WIDTH, ATT_WIDTH),
         ("av", ATT_WIDTH, ATT_WIDTH))
_PACK_ORDER = ("gq", "gk", "gv", "lr", "gg", "aq", "ca", "cb", "ak", "av")
IN_GROUP = 1024
SEG = {}
_off = 0
for _name in _PACK_ORDER:
    _wp = {s[0]: s[2] for s in _SEGS}[_name]
    assert _off // IN_GROUP == (_off + _wp - 1) // IN_GROUP
    SEG[_name] = (_off, _off + _wp)
    _off += _wp
D_IN_PAD = _off
assert D_IN_PAD % IN_GROUP == 0

F32 = jnp.float32
BF16 = jnp.bfloat16


def _dot(a, b):
    return jnp.dot(a, b, preferred_element_type=F32)


def _dot_nt(a, b):
    return lax.dot_general(a, b, (((1,), (1,)), ((), ())), preferred_element_type=F32)


def _dot_tn(a, b):
    return lax.dot_general(a, b, (((0,), (0,)), ((), ())), preferred_element_type=F32)


def _split_bf16(x):
    hi = x.astype(BF16)
    lo = (x - hi.astype(F32)).astype(BF16)
    return hi, lo


def _rmsnorm(x, g):
    return x * lax.rsqrt(jnp.mean(x * x, axis=-1, keepdims=True) + EPS) * g


def _mixer_kernel(h_ref, nmix_ref, win_ref, wgate_ref, bgate_ref, gnorm_ref, wdw_ref, bdw_ref, lng_ref, lnb_ref,
                  brow_ref, wout_ref, later_ref, hmask_ref, hsum_ref, o_ref, bias_ref, khist, vhist, ctail, cbuf, shifted,
                  st_ref):
    t = pl.program_id(1)

    @pl.when((pl.program_id(0) == 0) & (t == 0))
    def _build_bias_table():
        col = lax.broadcasted_iota(jnp.int32, (SUBLANES, KW), 1)
        for hd in range(ATT_HEADS):
            for a in range(QB // SUBLANES):
                start = BIAS_PERIOD - a * SUBLANES
                first_key = (a * SUBLANES // CHUNK) * CHUNK
                in_band = (col >= first_key) & (col < first_key + (ATT_LEFT_CHUNKS + 1) * CHUNK)
                bias_ref[hd, a * SUBLANES:(a + 1) * SUBLANES, :] = jnp.where(
                    in_band, brow_ref[hd, :, start:start + KW], NEG_INF)
    cur = lax.rem(t, 2)
    nxt = 1 - cur

    @pl.when(t == 0)
    def _start_of_sequence():
        khist[0] = jnp.zeros((HIST, ATT_WIDTH), BF16)
        vhist[0] = jnp.zeros((HIST, ATT_WIDTH), BF16)
        ctail[0] = jnp.zeros((CONV_PAD, CONV_WIDTH), F32)
        st_ref[...] = jnp.zeros_like(st_ref)

    h = h_ref[...]
    xn = _rmsnorm(h, nmix_ref[...]).astype(BF16)
    groups = {}

    def project(g):
        if g not in groups:
            groups[g] = _dot(xn, win_ref[:, g * IN_GROUP:(g + 1) * IN_GROUP])

    def proj(name):
        lo, hi = SEG[name]
        project(lo // IN_GROUP)
        return groups[lo // IN_GROUP][:, lo % IN_GROUP:(hi - 1) % IN_GROUP + 1]

    later_groups = [functools.partial(project, g) for g in range(1, D_IN_PAD // IN_GROUP)]
    o_gla = _gla_mixer(proj, wgate_ref, bgate_ref, gnorm_ref, later_ref, hmask_ref, hsum_ref, st_ref,
                       lambda: later_groups.pop(0)() if later_groups else None)
    o_conv = _conv_mixer(proj, cur, nxt, wdw_ref, bdw_ref, lng_ref, lnb_ref, ctail, cbuf, shifted)
    o_att = _attention_mixer(proj, t, cur, nxt, bias_ref, khist, vhist)
    mixed = jnp.concatenate([o_gla, o_conv, o_att], axis=1).astype(BF16)
    o_ref[...] = h + _dot(mixed, wout_ref[...])


def _gla_mixer(proj, wgate_ref, bgate_ref, gnorm_ref, later_ref, hmask_ref, hsum_ref, st_ref, between_links):
    gq = (proj("gq") * (GLA_DK ** -0.5)).astype(BF16)
    gk = proj("gk")
    gv = proj("gv").astype(BF16)
    z = _dot(proj("lr").astype(BF16), wgate_ref[...]) + bgate_ref[...]
    between_links()
    log_a = (jnp.minimum(z, 0.0) - jnp.log1p(jnp.exp(-jnp.abs(z)))) / GLA_GATE_TAU
    la_hi, la_lo = _split_bf16(log_a)
    later = later_ref[...]
    rev = _dot(later, la_hi) + _dot(later, la_lo)
    between_links()
    kdec = (gk * jnp.exp(rev)).astype(BF16)
    chunk_log = rev + log_a
    hmask = hmask_ref[...]
    chunks = [slice(c * CHUNK, (c + 1) * CHUNK) for c in range(NCH)]
    kv_t = [_dot_tn(gv[rows], kdec[rows]) * hmask for rows in chunks]
    decay = [jnp.exp(chunk_log[rows.start:rows.start + 1, :]) for rows in chunks]
    st = st_ref[...]
    states = []
    for c in range(NCH):
        st = st * decay[c] + kv_t[c]
        states.append(st.astype(BF16))
    st_ref[...] = st
    o_chunks = [_dot_nt(gq[rows], states[c]) for c, rows in enumerate(chunks)]
    og = jnp.concatenate(o_chunks, axis=0)
    sq_hi, sq_lo = _split_bf16(og * og)
    hsum = hsum_ref[...]
    ms = (_dot(sq_hi, hsum) + _dot(sq_lo, hsum)) / GLA_DV
    gg = proj("gg")
    return og * lax.rsqrt(ms + EPS) * gnorm_ref[...] * (gg * jax.nn.sigmoid(gg))


def _conv_mixer(proj, cur, nxt, wdw_ref, bdw_ref, lng_ref, lnb_ref, ctail, cbuf, shifted):
    hg = proj("ca") * jax.nn.sigmoid(proj("cb"))
    cbuf[0:CONV_PAD, :] = ctail[cur]
    cbuf[CONV_PAD:CONV_PAD + TS, :] = hg
    ctail[nxt] = hg[TS - CONV_PAD:TS, :]
    for s in range(1, SUBLANES):
        shifted[s - 1] = cbuf[pl.ds(s, SHIFT_ROWS), :]
    base = CONV_PAD - (CONV_KERNEL - 1)
    rows = []
    for rb in range(TS // CHUNK):
        acc = jnp.zeros((CHUNK, CONV_WIDTH), F32)
        for j in range(CONV_KERNEL):
            a, s = divmod(base + j, SUBLANES)
            src = cbuf if s == 0 else shifted.at[s - 1]
            acc = acc + src[pl.ds(rb * CHUNK + a * SUBLANES, CHUNK), :] * wdw_ref[j:j + 1, :]
        rows.append(acc)
    hc = jnp.concatenate(rows, axis=0) + bdw_ref[...]
    mu = jnp.mean(hc, axis=-1, keepdims=True)
    var = jnp.mean(jnp.square(hc - mu), axis=-1, keepdims=True)
    hc = (hc - mu) * lax.rsqrt(var + EPS) * lng_ref[...] + lnb_ref[...]
    return hc * jax.nn.sigmoid(hc)


def _attention_mixer(proj, t, cur, nxt, bias_ref, khist, vhist):
    aq = proj("aq") * (ATT_HEAD_DIM ** -0.5 * LOG2E)
    kb = proj("ak").astype(BF16)
    vb = proj("av").astype(BF16)
    khist[nxt] = kb
    vhist[nxt] = vb
    lane = lax.broadcasted_iota(jnp.int32, (1, LANES), 1)
    low_half = lane < ATT_HEAD_DIM
    ones_lane = (ATT_HEAD_DIM, 0)
    hist_neg = jnp.where(t == 0, NEG_INF, 0.0)
    one = jnp.ones((), BF16)
    v_ones = {}
    for p in range(ATT_HEADS // 2):
        ls = slice(p * LANES, (p + 1) * LANES)
        for hh, ol in enumerate(ones_lane):
            v_ones[p, hh] = (jnp.where(lane == ol, one, vhist[cur, :, ls]), jnp.where(lane == ol, one, vb[:, ls]))

    blocks = [(p, qb, hh) for p in range(ATT_HEADS // 2) for qb in range(TS // QB) for hh in range(2)]
    live = [{} for _ in blocks]

    def scores(n):
        p, qb, hh = blocks[n]
        ls = slice(p * LANES, (p + 1) * LANES)
        n_hist = HIST - qb * QB
        mine = low_half if hh == 0 else jnp.logical_not(low_half)
        qm = jnp.where(mine, aq[qb * QB:(qb + 1) * QB, ls], 0.0).astype(BF16)
        k_hist = khist[cur, qb * QB:HIST, ls]
        k_cur = kb[0:KW - n_hist, ls]
        live[n]["s"] = jnp.concatenate([_dot_nt(qm, k_hist) + hist_neg, _dot_nt(qm, k_cur)], axis=1)

    def row_max(n):
        p, _, hh = blocks[n]
        s = live[n].pop("s") + bias_ref[2 * p + hh]
        live[n]["s"] = s
        live[n]["m"] = jnp.max(s, axis=-1, keepdims=True)

    def exponent(n):
        live[n]["e"] = jnp.exp2(live[n].pop("s") - live[n].pop("m")).astype(BF16)

    def weighted_values(n):
        p, qb, hh = blocks[n]
        n_hist = HIST - qb * QB
        v_hist, v_cur = v_ones[p, hh][0][qb * QB:HIST], v_ones[p, hh][1][0:KW - n_hist]
        e = live[n].pop("e")
        live[n]["pv"] = _dot(e[:, :n_hist], v_hist) + _dot(e[:, n_hist:], v_cur)

    def normalise(n):
        hh = blocks[n][2]
        pv = live[n].pop("pv")
        live[n]["o"] = pv / pv[:, ones_lane[hh]:ones_lane[hh] + 1]

    stages = (scores, row_max, exponent, weighted_values, normalise)
    for step in range(len(blocks) + len(stages) - 1):
        for depth, stage in enumerate(stages):
            if 0 <= step - depth < len(blocks):
                stage(step - depth)

    o = {blk: live[n]["o"] for n, blk in enumerate(blocks)}
    return jnp.concatenate(
        [jnp.concatenate([jnp.where(low_half, o[p, qb, 0], o[p, qb, 1]) for qb in range(TS // QB)], axis=0)
         for p in range(ATT_HEADS // 2)], axis=1)


def _ffn_kernel(h_ref, nffn_ref, wup_ref, wdn_ref, nfin_ref, o_ref, *, final_norm):
    h = h_ref[...]
    xn = _rmsnorm(h, nffn_ref[...]).astype(BF16)
    acc = h
    for c in range(D_FF // FF_CHUNK):
        u = jnp.maximum(_dot(xn, wup_ref[:, c * FF_CHUNK:(c + 1) * FF_CHUNK]), 0.0)
        acc = acc + _dot((u * u).astype(BF16), wdn_ref[c * FF_CHUNK:(c + 1) * FF_CHUNK, :])
    if final_norm:
        acc = _rmsnorm(acc, nfin_ref[...])
    o_ref[...] = acc


def _const_spec(shape):
    nd = len(shape)
    return pl.BlockSpec(shape, lambda *_: (0,) * nd, pipeline_mode=pl.Buffered(1))


def _mixer_constants():
    i = np.arange(TS)
    later = ((i[None, :] > i[:, None]) & (i[None, :] // CHUNK == i[:, None] // CHUNK))
    v = np.arange(GLA_WIDTH)
    k = np.arange(KEY_PAD)
    hmask = ((v[:, None] // GLA_DV) == (k[None, :] // GLA_DK)) & (k[None, :] < GLA_KEY_WIDTH)
    hsum = (v[:, None] // GLA_DV) == (v[None, :] // GLA_DV)
    return (jnp.asarray(later, BF16), jnp.asarray(hmask, F32), jnp.asarray(hsum, BF16))


def _bias_rows(rel_bias):
    u = np.arange(BIAS_PERIOD)
    dist = np.where(u < KW, HIST - u, HIST + BIAS_PERIOD - u)
    w_idx = np.clip(dist, -MAX_REL_DIST, MAX_REL_DIST) + MAX_REL_DIST
    n_rel = 2 * MAX_REL_DIST + 1
    n_far = HIST - MAX_REL_DIST + 1
    n_mid = min(n_rel - 1, KW - n_far)
    n_near = KW - n_far - n_mid
    pieces = [np.full(n_far, n_rel - 1), np.arange(n_rel - 2, n_rel - 2 - n_mid, -1), np.zeros(n_near, int),
              np.full(BIAS_PERIOD - KW, n_rel - 1)]
    assert KW + QB <= BIAS_PERIOD and (w_idx == np.concatenate(pieces)).all()
    rb = rel_bias.astype(F32) * LOG2E
    lead = rb.shape[:-1]
    far, near = rb[..., n_rel - 1:n_rel], rb[..., 0:1]
    w = jnp.concatenate([jnp.broadcast_to(far, lead + (n_far,)), rb[..., n_rel - 2::-1][..., :n_mid],
                         jnp.broadcast_to(near, lead + (n_near,)),
                         jnp.broadcast_to(far, lead + (BIAS_PERIOD - KW,))], axis=-1)
    advanced = jnp.stack([jnp.roll(w, b, axis=-1) for b in range(SUBLANES)], axis=-2)
    return jnp.concatenate([advanced, advanced], axis=-1)


def _pad_last(w, width):
    return jnp.pad(w, [(0, 0)] * (w.ndim - 1) + [(0, width - w.shape[-1])])


def _pack_w_in(w_in):
    sizes = [s[1] for s in _SEGS]
    splits = np.cumsum(sizes)[:-1].tolist()
    parts = {s[0]: _pad_last(p, s[2]) for p, s in zip(jnp.split(w_in, splits, axis=-1), _SEGS)}
    return jnp.concatenate([parts[name] for name in _PACK_ORDER], axis=-1).astype(BF16)


def _layer_spec(shape, l):
    nd = len(shape) - 1
    return pl.BlockSpec((None,) + tuple(shape[1:]), lambda *_: (l,) + (0,) * nd, pipeline_mode=pl.Buffered(1))


def _mixer_layer(h, l, layered, shared):
    B, S, D = h.shape
    in_specs = [pl.BlockSpec((None, TS, D), lambda b, t: (b, t, 0))] + [_layer_spec(a.shape, l) for a in layered] \
        + [_const_spec(a.shape) for a in shared]
    return pl.pallas_call(
        _mixer_kernel,
        grid=(B, S // TS),
        in_specs=in_specs,
        out_specs=pl.BlockSpec((None, TS, D), lambda b, t: (b, t, 0)),
        out_shape=jax.ShapeDtypeStruct((B, S, D), F32),
        scratch_shapes=[pltpu.VMEM((ATT_HEADS, QB, KW), F32),
                        pltpu.VMEM((2, HIST, ATT_WIDTH), BF16), pltpu.VMEM((2, HIST, ATT_WIDTH), BF16),
                        pltpu.VMEM((2, CONV_PAD, CONV_WIDTH), F32),
                        pltpu.VMEM((CONV_PAD + TS, CONV_WIDTH), F32),
                        pltpu.VMEM((SUBLANES - 1, SHIFT_ROWS, CONV_WIDTH), F32),
                        pltpu.VMEM((GLA_WIDTH, KEY_PAD), F32)],
        compiler_params=pltpu.CompilerParams(dimension_semantics=("arbitrary", "arbitrary"),
                                             vmem_limit_bytes=VMEM_LIMIT_BYTES),
        name="mixer",
    )(h, *layered, *shared)


def _ffn_layer(h2, l, layered, nfin, final_norm):
    T, D = h2.shape
    in_specs = [pl.BlockSpec((TM, D), lambda i: (i, 0))] + [_layer_spec(a.shape, l) for a in layered] \
        + [_const_spec(nfin.shape)]
    return pl.pallas_call(
        functools.partial(_ffn_kernel, final_norm=final_norm),
        grid=(T // TM,),
        in_specs=in_specs,
        out_specs=pl.BlockSpec((TM, D), lambda i: (i, 0)),
        out_shape=jax.ShapeDtypeStruct((T, D), F32),
        compiler_params=pltpu.CompilerParams(dimension_semantics=("arbitrary",),
                                             vmem_limit_bytes=VMEM_LIMIT_BYTES),
        name="ffn",
    )(h2, *layered, nfin)


def kernel(x, norm_mix, w_in, w_gla_gate, b_gla_gate, gla_norm, w_dw, b_dw, conv_ln_g, conv_ln_b, rel_bias, w_out,
           norm_ffn, w_up, w_down, norm_final):
    B, S, D = x.shape
    assert (D, S % (2 * TS), (B * S) % TM) == (D_MODEL, 0, 0)
    rows = lambda a: a[:, None, :].astype(F32)
    wgate = jnp.pad(w_gla_gate, ((0, 0), (0, LR_PAD - GLA_GATE_RANK), (0, KEY_PAD - GLA_KEY_WIDTH))).astype(BF16)
    mixer_params = (rows(norm_mix), _pack_w_in(w_in), wgate, _pad_last(rows(b_gla_gate), KEY_PAD), rows(gla_norm),
                    w_dw.astype(F32), rows(b_dw), rows(conv_ln_g), rows(conv_ln_b), _bias_rows(rel_bias),
                    w_out.astype(BF16))
    ffn_params = (rows(norm_ffn), w_up.astype(BF16), w_down.astype(BF16))
    shared = _mixer_constants()
    nfin = norm_final[None, :].astype(F32)
    h = x
    for l in range(DEPTH):
        h = _mixer_layer(h, l, mixer_params, shared)
        h = _ffn_layer(h.reshape(B * S, D), l, ffn_params, nfin, final_norm=(l == DEPTH - 1)).reshape(B, S, D)
    return h
```

```python
import functools

import numpy as np
import jax
import jax.numpy as jnp
from jax import lax
from jax.experimental import pallas as pl
from jax.experimental.pallas import tpu as pltpu

D_MODEL = 1024
DEPTH = 2
CHUNK = 64
GLA_DV, GLA_DK, GLA_HEADS = 96, 48, 4
GLA_WIDTH = GLA_HEADS * GLA_DV
GLA_KEY_WIDTH = GLA_HEADS * GLA_DK
GLA_GATE_RANK = 16
GLA_GATE_TAU = 16.0
CONV_WIDTH = 256
CONV_KERNEL = 31
ATT_HEAD_DIM = 64
ATT_HEADS = 6
ATT_WIDTH = ATT_HEADS * ATT_HEAD_DIM
ATT_LEFT_CHUNKS = 8
MAX_REL_DIST = 128
D_FF = 4 * D_MODEL
EPS = 1e-6
LOG2E = 1.4426950408889634
NEG_INF = -1e30

LANES = 128
SUBLANES = 8
VMEM_LIMIT_BYTES = 56 * 1024 * 1024

TS = 512
NCH = TS // CHUNK
HIST = ATT_LEFT_CHUNKS * CHUNK
QB = 128
KW = QB + HIST
BIAS_PERIOD = 1024
CONV_PAD = 32
SHIFT_ROWS = TS + CONV_PAD - SUBLANES
TM = 1024
FF_CHUNK = 1024
UP_STAGE_ROWS = 128
DN_STAGE_ROWS = 512

KEY_PAD = 256
LR_PAD = LANES
_SEGS = (("gq", GLA_KEY_WIDTH, KEY_PAD), ("gk", GLA_KEY_WIDTH, KEY_PAD), ("gv", GLA_WIDTH, GLA_WIDTH),
         ("gg", GLA_WIDTH, GLA_WIDTH), ("lr", GLA_GATE_RANK, LR_PAD), ("ca", CONV_WIDTH, CONV_WIDTH),
         ("cb", CONV_WIDTH, CONV_WIDTH), ("aq", ATT_WIDTH, ATT_WIDTH), ("ak", ATT_WIDTH, ATT_WIDTH),
         ("av", ATT_WIDTH, ATT_WIDTH))
_PACK_ORDER = ("gq", "gk", "gv", "lr", "gg", "aq", "ca", "cb", "ak", "av")
IN_GROUP = 1024
SEG = {}
_off = 0
for _name in _PACK_ORDER:
    _wp = {s[0]: s[2] for s in _SEGS}[_name]
    assert _off // IN_GROUP == (_off + _wp - 1) // IN_GROUP
    SEG[_name] = (_off, _off + _wp)
    _off += _wp
D_IN_PAD = _off
assert D_IN_PAD % IN_GROUP == 0

F32 = jnp.float32
BF16 = jnp.bfloat16


def _dot(a, b):
    return jnp.dot(a, b, preferred_element_type=F32)


def _dot_nt(a, b):
    return lax.dot_general(a, b, (((1,), (1,)), ((), ())), preferred_element_type=F32)


def _dot_tn(a, b):
    return lax.dot_general(a, b, (((0,), (0,)), ((), ())), preferred_element_type=F32)


def _split_bf16(x):
    hi = x.astype(BF16)
    lo = (x - hi.astype(F32)).astype(BF16)
    return hi, lo


def _rmsnorm(x, g):
    return x * lax.rsqrt(jnp.mean(x * x, axis=-1, keepdims=True) + EPS) * g


def _mixer_kernel(h_ref, nmix_ref, win_ref, wgate_ref, bgate_ref, gnorm_ref, wdw_ref, bdw_ref, lng_ref, lnb_ref,
                  brow_ref, wout_ref, later_ref, hmask_ref, hsum_ref, o_ref, bias_ref, khist, vhist, ctail, cbuf, shifted,
                  st_ref):
    t = pl.program_id(1)

    @pl.when((pl.program_id(0) == 0) & (t == 0))
    def _build_bias_table():
        col = lax.broadcasted_iota(jnp.int32, (SUBLANES, KW), 1)
        for hd in range(ATT_HEADS):
            for a in range(QB // SUBLANES):
                start = BIAS_PERIOD - a * SUBLANES
                first_key = (a * SUBLANES // CHUNK) * CHUNK
                in_band = (col >= first_key) & (col < first_key + (ATT_LEFT_CHUNKS + 1) * CHUNK)
                bias_ref[hd, a * SUBLANES:(a + 1) * SUBLANES, :] = jnp.where(
                    in_band, brow_ref[hd, :, start:start + KW], NEG_INF)
    cur = lax.rem(t, 2)
    nxt = 1 - cur

    @pl.when(t == 0)
    def _start_of_sequence():
        khist[0] = jnp.zeros((HIST, ATT_WIDTH), BF16)
        vhist[0] = jnp.zeros((HIST, ATT_WIDTH), BF16)
        ctail[0] = jnp.zeros((CONV_PAD, CONV_WIDTH), F32)
        st_ref[...] = jnp.zeros_like(st_ref)

    h = h_ref[...]
    xn = _rmsnorm(h, nmix_ref[...]).astype(BF16)
    groups = {}

    def project(g):
        if g not in groups:
            groups[g] = _dot(xn, win_ref[:, g * IN_GROUP:(g + 1) * IN_GROUP])

    def proj(name):
        lo, hi = SEG[name]
        project(lo // IN_GROUP)
        return groups[lo // IN_GROUP][:, lo % IN_GROUP:(hi - 1) % IN_GROUP + 1]

    later_groups = [functools.partial(project, g) for g in range(1, D_IN_PAD // IN_GROUP)]
    o_gla = _gla_mixer(proj, wgate_ref, bgate_ref, gnorm_ref, later_ref, hmask_ref, hsum_ref, st_ref,
                       lambda: later_groups.pop(0)() if later_groups else None)
    o_conv = _conv_mixer(proj, cur, nxt, wdw_ref, bdw_ref, lng_ref, lnb_ref, ctail, cbuf, shifted)
    o_att = _attention_mixer(proj, t, cur, nxt, bias_ref, khist, vhist)
    mixed = jnp.concatenate([o_gla, o_conv, o_att], axis=1).astype(BF16)
    o_ref[...] = h + _dot(mixed, wout_ref[...])


def _gla_mixer(proj, wgate_ref, bgate_ref, gnorm_ref, later_ref, hmask_ref, hsum_ref, st_ref, between_links):
    gq = (proj("gq") * (GLA_DK ** -0.5)).astype(BF16)
    gk = proj("gk")
    gv = proj("gv").astype(BF16)
    z = _dot(proj("lr").astype(BF16), wgate_ref[...]) + bgate_ref[...]
    between_links()
    log_a = (jnp.minimum(z, 0.0) - jnp.log1p(jnp.exp(-jnp.abs(z)))) / GLA_GATE_TAU
    la_hi, la_lo = _split_bf16(log_a)
    later = later_ref[...]
    rev = _dot(later, la_hi) + _dot(later, la_lo)
    between_links()
    kdec = (gk * jnp.exp(rev)).astype(BF16)
    chunk_log = rev + log_a
    hmask = hmask_ref[...]
    chunks = [slice(c * CHUNK, (c + 1) * CHUNK) for c in range(NCH)]
    kv_t = [_dot_tn(gv[rows], kdec[rows]) * hmask for rows in chunks]
    decay = [jnp.exp(chunk_log[rows.start:rows.start + 1, :]) for rows in chunks]
    st = st_ref[...]
    states = []
    for c in range(NCH):
        st = st * decay[c] + kv_t[c]
        states.append(st.astype(BF16))
    st_ref[...] = st
    o_chunks = [_dot_nt(gq[rows], states[c]) for c, rows in enumerate(chunks)]
    og = jnp.concatenate(o_chunks, axis=0)
    sq_hi, sq_lo = _split_bf16(og * og)
    hsum = hsum_ref[...]
    ms = (_dot(sq_hi, hsum) + _dot(sq_lo, hsum)) / GLA_DV
    gg = proj("gg")
    return og * lax.rsqrt(ms + EPS) * gnorm_ref[...] * (gg * jax.nn.sigmoid(gg))


def _conv_mixer(proj, cur, nxt, wdw_ref, bdw_ref, lng_ref, lnb_ref, ctail, cbuf, shifted):
    hg = proj("ca") * jax.nn.sigmoid(proj("cb"))
    cbuf[0:CONV_PAD, :] = ctail[cur]
    cbuf[CONV_PAD:CONV_PAD + TS, :] = hg
    ctail[nxt] = hg[TS - CONV_PAD:TS, :]
    for s in range(1, SUBLANES):
        shifted[s - 1] = cbuf[pl.ds(s, SHIFT_ROWS), :]
    base = CONV_PAD - (CONV_KERNEL - 1)
    rows = []
    for rb in range(TS // CHUNK):
        acc = jnp.zeros((CHUNK, CONV_WIDTH), F32)
        for j in range(CONV_KERNEL):
            a, s = divmod(base + j, SUBLANES)
            src = cbuf if s == 0 else shifted.at[s - 1]
            acc = acc + src[pl.ds(rb * CHUNK + a * SUBLANES, CHUNK), :] * wdw_ref[j:j + 1, :]
        rows.append(acc)
    hc = jnp.concatenate(rows, axis=0) + bdw_ref[...]
    mu = jnp.mean(hc, axis=-1, keepdims=True)
    var = jnp.mean(jnp.square(hc - mu), axis=-1, keepdims=True)
    hc = (hc - mu) * lax.rsqrt(var + EPS) * lng_ref[...] + lnb_ref[...]
    return hc * jax.nn.sigmoid(hc)


def _attention_mixer(proj, t, cur, nxt, bias_ref, khist, vhist):
    aq = proj("aq") * (ATT_HEAD_DIM ** -0.5 * LOG2E)
    kb = proj("ak").astype(BF16)
    vb = proj("av").astype(BF16)
    khist[nxt] = kb
    vhist[nxt] = vb
    lane = lax.broadcasted_iota(jnp.int32, (1, LANES), 1)
    low_half = lane < ATT_HEAD_DIM
    ones_lane = (ATT_HEAD_DIM, 0)
    hist_neg = jnp.where(t == 0, NEG_INF, 0.0)
    one = jnp.ones((), BF16)
    v_ones = {}
    for p in range(ATT_HEADS // 2):
        ls = slice(p * LANES, (p + 1) * LANES)
        for hh, ol in enumerate(ones_lane):
            v_ones[p, hh] = (jnp.where(lane == ol, one, vhist[cur, :, ls]), jnp.where(lane == ol, one, vb[:, ls]))

    blocks = [(p, qb, hh) for p in range(ATT_HEADS // 2) for qb in range(TS // QB) for hh in range(2)]
    live = [{} for _ in blocks]

    def scores(n):
        p, qb, hh = blocks[n]
        ls = slice(p * LANES, (p + 1) * LANES)
        n_hist = HIST - qb * QB
        mine = low_half if hh == 0 else jnp.logical_not(low_half)
        qm = jnp.where(mine, aq[qb * QB:(qb + 1) * QB, ls], 0.0).astype(BF16)
        k_hist = khist[cur, qb * QB:HIST, ls]
        k_cur = kb[0:KW - n_hist, ls]
        live[n]["s"] = jnp.concatenate([_dot_nt(qm, k_hist) + hist_neg, _dot_nt(qm, k_cur)], axis=1)

    def row_max(n):
        p, _, hh = blocks[n]
        s = live[n].pop("s") + bias_ref[2 * p + hh]
        live[n]["s"] = s
        live[n]["m"] = jnp.max(s, axis=-1, keepdims=True)

    def exponent(n):
        live[n]["e"] = jnp.exp2(live[n].pop("s") - live[n].pop("m")).astype(BF16)

    def weighted_values(n):
        p, qb, hh = blocks[n]
        n_hist = HIST - qb * QB
        v_hist, v_cur = v_ones[p, hh][0][qb * QB:HIST], v_ones[p, hh][1][0:KW - n_hist]
        e = live[n].pop("e")
        live[n]["pv"] = _dot(e[:, :n_hist], v_hist) + _dot(e[:, n_hist:], v_cur)

    def normalise(n):
        hh = blocks[n][2]
        pv = live[n].pop("pv")
        live[n]["o"] = pv / pv[:, ones_lane[hh]:ones_lane[hh] + 1]

    stages = (scores, row_max, exponent, weighted_values, normalise)
    for step in range(len(blocks) + len(stages) - 1):
        for depth, stage in enumerate(stages):
            if 0 <= step - depth < len(blocks):
                stage(step - depth)

    o = {blk: live[n]["o"] for n, blk in enumerate(blocks)}
    return jnp.concatenate(
        [jnp.concatenate([jnp.where(low_half, o[p, qb, 0], o[p, qb, 1]) for qb in range(TS // QB)], axis=0)
         for p in range(ATT_HEADS // 2)], axis=1)


def _stream_cast(src_chunks, dst_chunks, stage, sem):
    copies = [pltpu.make_async_copy(src, stage.at[c % 2], sem.at[c % 2]) for c, src in enumerate(src_chunks)]
    copies[0].start()
    for c, dst in enumerate(dst_chunks):
        if c + 1 < len(copies):
            copies[c + 1].start()
        copies[c].wait()
        dst[...] = stage[c % 2].astype(BF16)


def _ffn_kernel(h_ref, nffn_ref, wup_hbm, wdn_hbm, nfin_ref, o_ref, wup_ref, wdn_ref, stage_up, stage_dn, sem_up,
                sem_dn, *, layer, final_norm):
    @pl.when(pl.program_id(0) == 0)
    def _load_weights():
        ups = range(0, D_MODEL, UP_STAGE_ROWS)
        _stream_cast([wup_hbm.at[layer, r:r + UP_STAGE_ROWS, :] for r in ups],
                     [wup_ref.at[r:r + UP_STAGE_ROWS, :] for r in ups], stage_up, sem_up)
        dns = range(0, D_FF, DN_STAGE_ROWS)
        _stream_cast([wdn_hbm.at[layer, r:r + DN_STAGE_ROWS, :] for r in dns],
                     [wdn_ref.at[r:r + DN_STAGE_ROWS, :] for r in dns], stage_dn, sem_dn)

    h = h_ref[...]
    xn = _rmsnorm(h, nffn_ref[...]).astype(BF16)
    acc = h
    for c in range(D_FF // FF_CHUNK):
        u = jnp.maximum(_dot(xn, wup_ref[:, c * FF_CHUNK:(c + 1) * FF_CHUNK]), 0.0)
        acc = acc + _dot((u * u).astype(BF16), wdn_ref[c * FF_CHUNK:(c + 1) * FF_CHUNK, :])
    if final_norm:
        acc = _rmsnorm(acc, nfin_ref[...])
    o_ref[...] = acc


def _const_spec(shape):
    nd = len(shape)
    return pl.BlockSpec(shape, lambda *_: (0,) * nd, pipeline_mode=pl.Buffered(1))


def _mixer_constants():
    i = np.arange(TS)
    later = ((i[None, :] > i[:, None]) & (i[None, :] // CHUNK == i[:, None] // CHUNK))
    v = np.arange(GLA_WIDTH)
    k = np.arange(KEY_PAD)
    hmask = ((v[:, None] // GLA_DV) == (k[None, :] // GLA_DK)) & (k[None, :] < GLA_KEY_WIDTH)
    hsum = (v[:, None] // GLA_DV) == (v[None, :] // GLA_DV)
    return (jnp.asarray(later, BF16), jnp.asarray(hmask, F32), jnp.asarray(hsum, BF16))


def _bias_rows(rel_bias):
    u = np.arange(BIAS_PERIOD)
    dist = np.where(u < KW, HIST - u, HIST + BIAS_PERIOD - u)
    w_idx = np.clip(dist, -MAX_REL_DIST, MAX_REL_DIST) + MAX_REL_DIST
    n_rel = 2 * MAX_REL_DIST + 1
    n_far = HIST - MAX_REL_DIST + 1
    n_mid = min(n_rel - 1, KW - n_far)
    n_near = KW - n_far - n_mid
    pieces = [np.full(n_far, n_rel - 1), np.arange(n_rel - 2, n_rel - 2 - n_mid, -1), np.zeros(n_near, int),
              np.full(BIAS_PERIOD - KW, n_rel - 1)]
    assert KW + QB <= BIAS_PERIOD and (w_idx == np.concatenate(pieces)).all()
    rb = rel_bias.astype(F32) * LOG2E
    lead = rb.shape[:-1]
    far, near = rb[..., n_rel - 1:n_rel], rb[..., 0:1]
    w = jnp.concatenate([jnp.broadcast_to(far, lead + (n_far,)), rb[..., n_rel - 2::-1][..., :n_mid],
                         jnp.broadcast_to(near, lead + (n_near,)),
                         jnp.broadcast_to(far, lead + (BIAS_PERIOD - KW,))], axis=-1)
    advanced = jnp.stack([jnp.roll(w, b, axis=-1) for b in range(SUBLANES)], axis=-2)
    return jnp.concatenate([advanced, advanced], axis=-1)


def _pad_last(w, width):
    return jnp.pad(w, [(0, 0)] * (w.ndim - 1) + [(0, width - w.shape[-1])])


def _pack_w_in(w_in):
    sizes = [s[1] for s in _SEGS]
    splits = np.cumsum(sizes)[:-1].tolist()
    parts = {s[0]: _pad_last(p, s[2]) for p, s in zip(jnp.split(w_in, splits, axis=-1), _SEGS)}
    return jnp.concatenate([parts[name] for name in _PACK_ORDER], axis=-1).astype(BF16)


def _layer_spec(shape, l):
    nd = len(shape) - 1
    return pl.BlockSpec((None,) + tuple(shape[1:]), lambda *_: (l,) + (0,) * nd, pipeline_mode=pl.Buffered(1))


def _mixer_layer(h, l, layered, shared):
    B, S, D = h.shape
    in_specs = [pl.BlockSpec((None, TS, D), lambda b, t: (b, t, 0))] + [_layer_spec(a.shape, l) for a in layered] \
        + [_const_spec(a.shape) for a in shared]
    return pl.pallas_call(
        _mixer_kernel,
        grid=(B, S // TS),
        in_specs=in_specs,
        out_specs=pl.BlockSpec((None, TS, D), lambda b, t: (b, t, 0)),
        out_shape=jax.ShapeDtypeStruct((B, S, D), F32),
        scratch_shapes=[pltpu.VMEM((ATT_HEADS, QB, KW), F32),
                        pltpu.VMEM((2, HIST, ATT_WIDTH), BF16), pltpu.VMEM((2, HIST, ATT_WIDTH), BF16),
                        pltpu.VMEM((2, CONV_PAD, CONV_WIDTH), F32),
                        pltpu.VMEM((CONV_PAD + TS, CONV_WIDTH), F32),
                        pltpu.VMEM((SUBLANES - 1, SHIFT_ROWS, CONV_WIDTH), F32),
                        pltpu.VMEM((GLA_WIDTH, KEY_PAD), F32)],
        compiler_params=pltpu.CompilerParams(dimension_semantics=("arbitrary", "arbitrary"),
                                             vmem_limit_bytes=VMEM_LIMIT_BYTES),
        name="mixer",
    )(h, *layered, *shared)


def _ffn_layer(h2, l, nffn, w_up, w_down, nfin, final_norm):
    T, D = h2.shape
    hbm = pl.BlockSpec(memory_space=pl.ANY)
    in_specs = [pl.BlockSpec((TM, D), lambda i: (i, 0)), _layer_spec(nffn.shape, l), hbm, hbm, _const_spec(nfin.shape)]
    return pl.pallas_call(
        functools.partial(_ffn_kernel, layer=l, final_norm=final_norm),
        grid=(T // TM,),
        in_specs=in_specs,
        out_specs=pl.BlockSpec((TM, D), lambda i: (i, 0)),
        out_shape=jax.ShapeDtypeStruct((T, D), F32),
        scratch_shapes=[pltpu.VMEM((D_MODEL, D_FF), BF16), pltpu.VMEM((D_FF, D_MODEL), BF16),
                        pltpu.VMEM((2, UP_STAGE_ROWS, D_FF), F32), pltpu.VMEM((2, DN_STAGE_ROWS, D_MODEL), F32),
                        pltpu.SemaphoreType.DMA((2,)), pltpu.SemaphoreType.DMA((2,))],
        compiler_params=pltpu.CompilerParams(dimension_semantics=("arbitrary",),
                                             vmem_limit_bytes=VMEM_LIMIT_BYTES),
        name="ffn",
    )(h2, nffn, w_up, w_down, nfin)


def kernel(x, norm_mix, w_in, w_gla_gate, b_gla_gate, gla_norm, w_dw, b_dw, conv_ln_g, conv_ln_b, rel_bias, w_out,
           norm_ffn, w_up, w_down, norm_final):
    B, S, D = x.shape
    assert (D, S % (2 * TS), (B * S) % TM) == (D_MODEL, 0, 0)
    rows = lambda a: a[:, None, :].astype(F32)
    wgate = jnp.pad(w_gla_gate, ((0, 0), (0, LR_PAD - GLA_GATE_RANK), (0, KEY_PAD - GLA_KEY_WIDTH))).astype(BF16)
    mixer_params = (rows(norm_mix), _pack_w_in(w_in), wgate, _pad_last(rows(b_gla_gate), KEY_PAD), rows(gla_norm),
                    w_dw.astype(F32), rows(b_dw), rows(conv_ln_g), rows(conv_ln_b), _bias_rows(rel_bias),
                    w_out.astype(BF16))
    nffn = rows(norm_ffn)
    shared = _mixer_constants()
    nfin = norm_final[None, :].astype(F32)
    h = x
    for l in range(DEPTH):
        h = _mixer_layer(h, l, mixer_params, shared)
        h = _ffn_layer(h.reshape(B * S, D), l, nffn, w_up, w_down, nfin,
                       final_norm=(l == DEPTH - 1)).reshape(B, S, D)
    return h
```

```python
import functools

import numpy as np
import jax
import jax.numpy as jnp
from jax import lax
from jax.experimental import pallas as pl
from jax.experimental.pallas import tpu as pltpu

D_MODEL = 1024
DEPTH = 2
CHUNK = 64
GLA_DV, GLA_DK, GLA_HEADS = 96, 48, 4
GLA_WIDTH = GLA_HEADS * GLA_DV
GLA_KEY_WIDTH = GLA_HEADS * GLA_DK
GLA_GATE_RANK = 16
GLA_GATE_TAU = 16.0
CONV_WIDTH = 256
CONV_KERNEL = 31
ATT_HEAD_DIM = 64
ATT_HEADS = 6
ATT_WIDTH = ATT_HEADS * ATT_HEAD_DIM
ATT_LEFT_CHUNKS = 8
MAX_REL_DIST = 128
D_FF = 4 * D_MODEL
EPS = 1e-6
LOG2E = 1.4426950408889634
NEG_INF = -1e30

LANES = 128
SUBLANES = 8
BF16_ROWS = 2 * SUBLANES
VMEM_LIMIT_BYTES = 56 * 1024 * 1024

TS = 512
NCH = TS // CHUNK
HIST = ATT_LEFT_CHUNKS * CHUNK
QB = 128
KW = QB + HIST
BIAS_PERIOD = 1024
CONV_PAD = 32
SHIFT_ROWS = TS + CONV_PAD - SUBLANES
TM = 1024
FF_CHUNK = 1024

KEY_PAD = 256
LR_PAD = LANES
_SEGS = (("gq", GLA_KEY_WIDTH, KEY_PAD), ("gk", GLA_KEY_WIDTH, KEY_PAD), ("gv", GLA_WIDTH, GLA_WIDTH),
         ("gg", GLA_WIDTH, GLA_WIDTH), ("lr", GLA_GATE_RANK, LR_PAD), ("ca", CONV_WIDTH, CONV_WIDTH),
         ("cb", CONV_WIDTH, CONV_WIDTH), ("aq", ATT_WIDTH, ATT_WIDTH), ("ak", ATT_WIDTH, ATT_WIDTH),
         ("av", ATT_WIDTH, ATT_WIDTH))
_PACK_ORDER = ("gq", "gk", "gv", "lr", "gg", "aq", "ca", "cb", "ak", "av")
IN_GROUP = 1024
SEG = {}
_off = 0
for _name in _PACK_ORDER:
    _wp = {s[0]: s[2] for s in _SEGS}[_name]
    assert _off // IN_GROUP == (_off + _wp - 1) // IN_GROUP
    SEG[_name] = (_off, _off + _wp)
    _off += _wp
D_IN_PAD = _off
assert D_IN_PAD % IN_GROUP == 0

F32 = jnp.float32
BF16 = jnp.bfloat16


def _dot(a, b):
    return jnp.dot(a, b, preferred_element_type=F32)


def _dot_nt(a, b):
    return lax.dot_general(a, b, (((1,), (1,)), ((), ())), preferred_element_type=F32)


def _dot_tn(a, b):
    return lax.dot_general(a, b, (((0,), (0,)), ((), ())), preferred_element_type=F32)


def _split_bf16(x):
    hi = x.astype(BF16)
    lo = (x - hi.astype(F32)).astype(BF16)
    return hi, lo


def _rmsnorm(x, g):
    return x * lax.rsqrt(jnp.mean(x * x, axis=-1, keepdims=True) + EPS) * g


def _mixer_kernel(h_ref, nmix_ref, win_ref, wgate_ref, bgate_ref, gnorm_ref, wdw_ref, bdw_ref, lng_ref, lnb_ref,
                  brow_ref, wout_ref, later_ref, hmask_ref, hsum_ref, wup_ref, wdn_ref, o_ref, wup_bf_ref, wdn_bf_ref,
                  bias_ref, khist, vhist, ctail, cbuf, shifted, st_ref):
    t = pl.program_id(1)
    wup_bf_ref[...] = wup_ref[...].astype(BF16)
    wdn_bf_ref[...] = wdn_ref[...].astype(BF16)

    @pl.when((pl.program_id(0) == 0) & (t == 0))
    def _build_bias_table():
        col = lax.broadcasted_iota(jnp.int32, (SUBLANES, KW), 1)
        for hd in range(ATT_HEADS):
            for a in range(QB // SUBLANES):
                start = BIAS_PERIOD - a * SUBLANES
                first_key = (a * SUBLANES // CHUNK) * CHUNK
                in_band = (col >= first_key) & (col < first_key + (ATT_LEFT_CHUNKS + 1) * CHUNK)
                bias_ref[hd, a * SUBLANES:(a + 1) * SUBLANES, :] = jnp.where(
                    in_band, brow_ref[hd, :, start:start + KW], NEG_INF)
    cur = lax.rem(t, 2)
    nxt = 1 - cur

    @pl.when(t == 0)
    def _start_of_sequence():
        khist[0] = jnp.zeros((HIST, ATT_WIDTH), BF16)
        vhist[0] = jnp.zeros((HIST, ATT_WIDTH), BF16)
        ctail[0] = jnp.zeros((CONV_PAD, CONV_WIDTH), F32)
        st_ref[...] = jnp.zeros_like(st_ref)

    h = h_ref[...]
    xn = _rmsnorm(h, nmix_ref[...]).astype(BF16)
    groups = {}

    def project(g):
        if g not in groups:
            groups[g] = _dot(xn, win_ref[:, g * IN_GROUP:(g + 1) * IN_GROUP])

    def proj(name):
        lo, hi = SEG[name]
        project(lo // IN_GROUP)
        return groups[lo // IN_GROUP][:, lo % IN_GROUP:(hi - 1) % IN_GROUP + 1]

    later_groups = [functools.partial(project, g) for g in range(1, D_IN_PAD // IN_GROUP)]
    o_gla = _gla_mixer(proj, wgate_ref, bgate_ref, gnorm_ref, later_ref, hmask_ref, hsum_ref, st_ref,
                       lambda: later_groups.pop(0)() if later_groups else None)
    o_conv = _conv_mixer(proj, cur, nxt, wdw_ref, bdw_ref, lng_ref, lnb_ref, ctail, cbuf, shifted)
    o_att = _attention_mixer(proj, t, cur, nxt, bias_ref, khist, vhist)
    mixed = jnp.concatenate([o_gla, o_conv, o_att], axis=1).astype(BF16)
    o_ref[...] = h + _dot(mixed, wout_ref[...])


def _gla_mixer(proj, wgate_ref, bgate_ref, gnorm_ref, later_ref, hmask_ref, hsum_ref, st_ref, between_links):
    gq = (proj("gq") * (GLA_DK ** -0.5)).astype(BF16)
    gk = proj("gk")
    gv = proj("gv").astype(BF16)
    z = _dot(proj("lr").astype(BF16), wgate_ref[...]) + bgate_ref[...]
    between_links()
    log_a = (jnp.minimum(z, 0.0) - jnp.log1p(jnp.exp(-jnp.abs(z)))) / GLA_GATE_TAU
    la_hi, la_lo = _split_bf16(log_a)
    later = later_ref[...]
    rev = _dot(later, la_hi) + _dot(later, la_lo)
    between_links()
    kdec = (gk * jnp.exp(rev)).astype(BF16)
    chunk_log = rev + log_a
    hmask = hmask_ref[...]
    chunks = [slice(c * CHUNK, (c + 1) * CHUNK) for c in range(NCH)]
    kv_t = [_dot_tn(gv[rows], kdec[rows]) * hmask for rows in chunks]
    decay = [jnp.exp(chunk_log[rows.start:rows.start + 1, :]) for rows in chunks]
    st = st_ref[...]
    states = []
    for c in range(NCH):
        st = st * decay[c] + kv_t[c]
        states.append(st.astype(BF16))
    st_ref[...] = st
    o_chunks = [_dot_nt(gq[rows], states[c]) for c, rows in enumerate(chunks)]
    og = jnp.concatenate(o_chunks, axis=0)
    sq_hi, sq_lo = _split_bf16(og * og)
    hsum = hsum_ref[...]
    ms = (_dot(sq_hi, hsum) + _dot(sq_lo, hsum)) / GLA_DV
    gg = proj("gg")
    return og * lax.rsqrt(ms + EPS) * gnorm_ref[...] * (gg * jax.nn.sigmoid(gg))


def _conv_mixer(proj, cur, nxt, wdw_ref, bdw_ref, lng_ref, lnb_ref, ctail, cbuf, shifted):
    hg = proj("ca") * jax.nn.sigmoid(proj("cb"))
    cbuf[0:CONV_PAD, :] = ctail[cur]
    cbuf[CONV_PAD:CONV_PAD + TS, :] = hg
    ctail[nxt] = hg[TS - CONV_PAD:TS, :]
    for s in range(1, SUBLANES):
        shifted[s - 1] = cbuf[pl.ds(s, SHIFT_ROWS), :]
    base = CONV_PAD - (CONV_KERNEL - 1)
    rows = []
    for rb in range(TS // CHUNK):
        acc = jnp.zeros((CHUNK, CONV_WIDTH), F32)
        for j in range(CONV_KERNEL):
            a, s = divmod(base + j, SUBLANES)
            src = cbuf if s == 0 else shifted.at[s - 1]
            acc = acc + src[pl.ds(rb * CHUNK + a * SUBLANES, CHUNK), :] * wdw_ref[j:j + 1, :]
        rows.append(acc)
    hc = jnp.concatenate(rows, axis=0) + bdw_ref[...]
    mu = jnp.mean(hc, axis=-1, keepdims=True)
    var = jnp.mean(jnp.square(hc - mu), axis=-1, keepdims=True)
    hc = (hc - mu) * lax.rsqrt(var + EPS) * lng_ref[...] + lnb_ref[...]
    return hc * jax.nn.sigmoid(hc)


def _attention_mixer(proj, t, cur, nxt, bias_ref, khist, vhist):
    aq = proj("aq") * (ATT_HEAD_DIM ** -0.5 * LOG2E)
    kb = proj("ak").astype(BF16)
    vb = proj("av").astype(BF16)
    khist[nxt] = kb
    vhist[nxt] = vb
    lane = lax.broadcasted_iota(jnp.int32, (1, LANES), 1)
    low_half = lane < ATT_HEAD_DIM
    ones_lane = (ATT_HEAD_DIM, 0)
    hist_neg = jnp.where(t == 0, NEG_INF, 0.0)
    one = jnp.ones((), BF16)
    v_ones = {}
    for p in range(ATT_HEADS // 2):
        ls = slice(p * LANES, (p + 1) * LANES)
        for hh, ol in enumerate(ones_lane):
            v_ones[p, hh] = (jnp.where(lane == ol, one, vhist[cur, :, ls]), jnp.where(lane == ol, one, vb[:, ls]))

    blocks = [(p, qb, hh) for p in range(ATT_HEADS // 2) for qb in range(TS // QB) for hh in range(2)]
    live = [{} for _ in blocks]

    def scores(n):
        p, qb, hh = blocks[n]
        ls = slice(p * LANES, (p + 1) * LANES)
        n_hist = HIST - qb * QB
        mine = low_half if hh == 0 else jnp.logical_not(low_half)
        qm = jnp.where(mine, aq[qb * QB:(qb + 1) * QB, ls], 0.0).astype(BF16)
        k_hist = khist[cur, qb * QB:HIST, ls]
        k_cur = kb[0:KW - n_hist, ls]
        live[n]["s"] = jnp.concatenate([_dot_nt(qm, k_hist) + hist_neg, _dot_nt(qm, k_cur)], axis=1)

    def row_max(n):
        p, _, hh = blocks[n]
        s = live[n].pop("s") + bias_ref[2 * p + hh]
        live[n]["s"] = s
        live[n]["m"] = jnp.max(s, axis=-1, keepdims=True)

    def exponent(n):
        live[n]["e"] = jnp.exp2(live[n].pop("s") - live[n].pop("m")).astype(BF16)

    def weighted_values(n):
        p, qb, hh = blocks[n]
        n_hist = HIST - qb * QB
        v_hist, v_cur = v_ones[p, hh][0][qb * QB:HIST], v_ones[p, hh][1][0:KW - n_hist]
        e = live[n].pop("e")
        live[n]["pv"] = _dot(e[:, :n_hist], v_hist) + _dot(e[:, n_hist:], v_cur)

    def normalise(n):
        hh = blocks[n][2]
        pv = live[n].pop("pv")
        live[n]["o"] = pv / pv[:, ones_lane[hh]:ones_lane[hh] + 1]

    stages = (scores, row_max, exponent, weighted_values, normalise)
    for step in range(len(blocks) + len(stages) - 1):
        for depth, stage in enumerate(stages):
            if 0 <= step - depth < len(blocks):
                stage(step - depth)

    o = {blk: live[n]["o"] for n, blk in enumerate(blocks)}
    return jnp.concatenate(
        [jnp.concatenate([jnp.where(low_half, o[p, qb, 0], o[p, qb, 1]) for qb in range(TS // QB)], axis=0)
         for p in range(ATT_HEADS // 2)], axis=1)


def _ffn_kernel(h_ref, nffn_ref, wup_ref, wdn_ref, nfin_ref, o_ref, *, final_norm):
    h = h_ref[...]
    xn = _rmsnorm(h, nffn_ref[...]).astype(BF16)
    acc = h
    for c in range(D_FF // FF_CHUNK):
        u = jnp.maximum(_dot(xn, wup_ref[:, c * FF_CHUNK:(c + 1) * FF_CHUNK]), 0.0)
        acc = acc + _dot((u * u).astype(BF16), wdn_ref[c * FF_CHUNK:(c + 1) * FF_CHUNK, :])
    if final_norm:
        acc = _rmsnorm(acc, nfin_ref[...])
    o_ref[...] = acc


def _const_spec(shape):
    nd = len(shape)
    return pl.BlockSpec(shape, lambda *_: (0,) * nd, pipeline_mode=pl.Buffered(1))


def _mixer_constants():
    i = np.arange(TS)
    later = ((i[None, :] > i[:, None]) & (i[None, :] // CHUNK == i[:, None] // CHUNK))
    v = np.arange(GLA_WIDTH)
    k = np.arange(KEY_PAD)
    hmask = ((v[:, None] // GLA_DV) == (k[None, :] // GLA_DK)) & (k[None, :] < GLA_KEY_WIDTH)
    hsum = (v[:, None] // GLA_DV) == (v[None, :] // GLA_DV)
    return (jnp.asarray(later, BF16), jnp.asarray(hmask, F32), jnp.asarray(hsum, BF16))


def _bias_rows(rel_bias):
    u = np.arange(BIAS_PERIOD)
    dist = np.where(u < KW, HIST - u, HIST + BIAS_PERIOD - u)
    w_idx = np.clip(dist, -MAX_REL_DIST, MAX_REL_DIST) + MAX_REL_DIST
    n_rel = 2 * MAX_REL_DIST + 1
    n_far = HIST - MAX_REL_DIST + 1
    n_mid = min(n_rel - 1, KW - n_far)
    n_near = KW - n_far - n_mid
    pieces = [np.full(n_far, n_rel - 1), np.arange(n_rel - 2, n_rel - 2 - n_mid, -1), np.zeros(n_near, int),
              np.full(BIAS_PERIOD - KW, n_rel - 1)]
    assert KW + QB <= BIAS_PERIOD and (w_idx == np.concatenate(pieces)).all()
    rb = rel_bias.astype(F32) * LOG2E
    lead = rb.shape[:-1]
    far, near = rb[..., n_rel - 1:n_rel], rb[..., 0:1]
    w = jnp.concatenate([jnp.broadcast_to(far, lead + (n_far,)), rb[..., n_rel - 2::-1][..., :n_mid],
                         jnp.broadcast_to(near, lead + (n_near,)),
                         jnp.broadcast_to(far, lead + (BIAS_PERIOD - KW,))], axis=-1)
    advanced = jnp.stack([jnp.roll(w, b, axis=-1) for b in range(SUBLANES)], axis=-2)
    return jnp.concatenate([advanced, advanced], axis=-1)


def _pad_last(w, width):
    return jnp.pad(w, [(0, 0)] * (w.ndim - 1) + [(0, width - w.shape[-1])])


def _pack_w_in(w_in):
    sizes = [s[1] for s in _SEGS]
    splits = np.cumsum(sizes)[:-1].tolist()
    parts = {s[0]: _pad_last(p, s[2]) for p, s in zip(jnp.split(w_in, splits, axis=-1), _SEGS)}
    return jnp.concatenate([parts[name] for name in _PACK_ORDER], axis=-1).astype(BF16)


def _layer_spec(shape, l):
    nd = len(shape) - 1
    return pl.BlockSpec((None,) + tuple(shape[1:]), lambda *_: (l,) + (0,) * nd, pipeline_mode=pl.Buffered(1))


def _mixer_layer(h, l, layered, shared, w_up, w_down):
    B, S, D = h.shape
    nt = S // TS
    steps = B * nt
    up_rows, dn_rows = D_MODEL // steps, D_FF // steps
    assert up_rows % BF16_ROWS == 0 and dn_rows % BF16_ROWS == 0
    in_specs = [pl.BlockSpec((None, TS, D), lambda b, t: (b, t, 0))] + [_layer_spec(a.shape, l) for a in layered] \
        + [_const_spec(a.shape) for a in shared] \
        + [pl.BlockSpec((None, up_rows, D_FF), lambda b, t: (l, b * nt + t, 0)),
           pl.BlockSpec((None, dn_rows, D_MODEL), lambda b, t: (l, b * nt + t, 0))]
    return pl.pallas_call(
        _mixer_kernel,
        grid=(B, nt),
        in_specs=in_specs,
        out_specs=[pl.BlockSpec((None, TS, D), lambda b, t: (b, t, 0)),
                   pl.BlockSpec((up_rows, D_FF), lambda b, t: (b * nt + t, 0)),
                   pl.BlockSpec((dn_rows, D_MODEL), lambda b, t: (b * nt + t, 0))],
        out_shape=[jax.ShapeDtypeStruct((B, S, D), F32), jax.ShapeDtypeStruct((D_MODEL, D_FF), BF16),
                   jax.ShapeDtypeStruct((D_FF, D_MODEL), BF16)],
        scratch_shapes=[pltpu.VMEM((ATT_HEADS, QB, KW), F32),
                        pltpu.VMEM((2, HIST, ATT_WIDTH), BF16), pltpu.VMEM((2, HIST, ATT_WIDTH), BF16),
                        pltpu.VMEM((2, CONV_PAD, CONV_WIDTH), F32),
                        pltpu.VMEM((CONV_PAD + TS, CONV_WIDTH), F32),
                        pltpu.VMEM((SUBLANES - 1, SHIFT_ROWS, CONV_WIDTH), F32),
                        pltpu.VMEM((GLA_WIDTH, KEY_PAD), F32)],
        compiler_params=pltpu.CompilerParams(dimension_semantics=("arbitrary", "arbitrary"),
                                             vmem_limit_bytes=VMEM_LIMIT_BYTES),
        name="mixer",
    )(h, *layered, *shared, w_up, w_down)


def _ffn_layer(h2, l, nffn, wup_bf, wdn_bf, nfin, final_norm):
    T, D = h2.shape
    in_specs = [pl.BlockSpec((TM, D), lambda i: (i, 0)), _layer_spec(nffn.shape, l), _const_spec(wup_bf.shape),
                _const_spec(wdn_bf.shape), _const_spec(nfin.shape)]
    return pl.pallas_call(
        functools.partial(_ffn_kernel, final_norm=final_norm),
        grid=(T // TM,),
        in_specs=in_specs,
        out_specs=pl.BlockSpec((TM, D), lambda i: (i, 0)),
        out_shape=jax.ShapeDtypeStruct((T, D), F32),
        compiler_params=pltpu.CompilerParams(dimension_semantics=("arbitrary",),
                                             vmem_limit_bytes=VMEM_LIMIT_BYTES),
        name="ffn",
    )(h2, nffn, wup_bf, wdn_bf, nfin)


def kernel(x, norm_mix, w_in, w_gla_gate, b_gla_gate, gla_norm, w_dw, b_dw, conv_ln_g, conv_ln_b, rel_bias, w_out,
           norm_ffn, w_up, w_down, norm_final):
    B, S, D = x.shape
    assert (D, S % (2 * TS), (B * S) % TM) == (D_MODEL, 0, 0)
    rows = lambda a: a[:, None, :].astype(F32)
    wgate = jnp.pad(w_gla_gate, ((0, 0), (0, LR_PAD - GLA_GATE_RANK), (0, KEY_PAD - GLA_KEY_WIDTH))).astype(BF16)
    mixer_params = (rows(norm_mix), _pack_w_in(w_in), wgate, _pad_last(rows(b_gla_gate), KEY_PAD), rows(gla_norm),
                    w_dw.astype(F32), rows(b_dw), rows(conv_ln_g), rows(conv_ln_b), _bias_rows(rel_bias),
                    w_out.astype(BF16))
    nffn = rows(norm_ffn)
    shared = _mixer_constants()
    nfin = norm_final[None, :].astype(F32)
    h = x
    for l in range(DEPTH):
        h, wup_bf, wdn_bf = _mixer_layer(h, l, mixer_params, shared, w_up, w_down)
        h = _ffn_layer(h.reshape(B * S, D), l, nffn, wup_bf, wdn_bf, nfin,
                       final_norm=(l == DEPTH - 1)).reshape(B, S, D)
    return h
```

```python
import functools

import numpy as np
import jax
import jax.numpy as jnp
from jax import lax
from jax.experimental import pallas as pl
from jax.experimental.pallas import tpu as pltpu

D_MODEL = 1024
DEPTH = 2
CHUNK = 64
GLA_DV, GLA_DK, GLA_HEADS = 96, 48, 4
GLA_WIDTH = GLA_HEADS * GLA_DV
GLA_KEY_WIDTH = GLA_HEADS * GLA_DK
GLA_GATE_RANK = 16
GLA_GATE_TAU = 16.0
CONV_WIDTH = 256
CONV_KERNEL = 31
ATT_HEAD_DIM = 64
ATT_HEADS = 6
ATT_WIDTH = ATT_HEADS * ATT_HEAD_DIM
ATT_LEFT_CHUNKS = 8
MAX_REL_DIST = 128
D_FF = 4 * D_MODEL
EPS = 1e-6
LOG2E = 1.4426950408889634
NEG_INF = -1e30

LANES = 128
SUBLANES = 8
BF16_ROWS = 2 * SUBLANES
VMEM_LIMIT_BYTES = 56 * 1024 * 1024

TS = 512
NCH = TS // CHUNK
HIST = ATT_LEFT_CHUNKS * CHUNK
QB = 128
KW = QB + HIST
BIAS_PERIOD = 1024
CONV_PAD = 32
SHIFT_ROWS = TS + CONV_PAD - SUBLANES
TM = 1024
FFN_IN_BUFS = 3
FFN_OUT_BUFS = 2
FF_CHUNK = 1024

KEY_PAD = 256
LR_PAD = LANES
_SEGS = (("gq", GLA_KEY_WIDTH, KEY_PAD), ("gk", GLA_KEY_WIDTH, KEY_PAD), ("gv", GLA_WIDTH, GLA_WIDTH),
         ("gg", GLA_WIDTH, GLA_WIDTH), ("lr", GLA_GATE_RANK, LR_PAD), ("ca", CONV_WIDTH, CONV_WIDTH),
         ("cb", CONV_WIDTH, CONV_WIDTH), ("aq", ATT_WIDTH, ATT_WIDTH), ("ak", ATT_WIDTH, ATT_WIDTH),
         ("av", ATT_WIDTH, ATT_WIDTH))
_PACK_ORDER = ("gq", "gk", "gv", "lr", "gg", "aq", "ca", "cb", "ak", "av")
IN_GROUP = 1024
SEG = {}
_off = 0
for _name in _PACK_ORDER:
    _wp = {s[0]: s[2] for s in _SEGS}[_name]
    assert _off // IN_GROUP == (_off + _wp - 1) // IN_GROUP
    SEG[_name] = (_off, _off + _wp)
    _off += _wp
D_IN_PAD = _off
assert D_IN_PAD % IN_GROUP == 0

F32 = jnp.float32
BF16 = jnp.bfloat16


def _dot(a, b):
    return jnp.dot(a, b, preferred_element_type=F32)


def _dot_nt(a, b):
    return lax.dot_general(a, b, (((1,), (1,)), ((), ())), preferred_element_type=F32)


def _dot_tn(a, b):
    return lax.dot_general(a, b, (((0,), (0,)), ((), ())), preferred_element_type=F32)


def _split_bf16(x):
    hi = x.astype(BF16)
    lo = (x - hi.astype(F32)).astype(BF16)
    return hi, lo


def _rmsnorm(x, g):
    return x * lax.rsqrt(jnp.mean(x * x, axis=-1, keepdims=True) + EPS) * g


def _mixer_kernel(h_ref, nmix_ref, win_ref, wgate_ref, bgate_ref, gnorm_ref, wdw_ref, bdw_ref, lng_ref, lnb_ref,
                  brow_ref, wout_ref, later_ref, hmask_ref, hsum_ref, wup_ref, wdn_ref, o_ref, wup_bf_ref, wdn_bf_ref,
                  bias_ref, khist, vhist, ctail, cbuf, shifted, st_ref):
    t = pl.program_id(1)
    wup_bf_ref[...] = wup_ref[...].astype(BF16)
    wdn_bf_ref[...] = wdn_ref[...].astype(BF16)

    @pl.when((pl.program_id(0) == 0) & (t == 0))
    def _build_bias_table():
        col = lax.broadcasted_iota(jnp.int32, (SUBLANES, KW), 1)
        for hd in range(ATT_HEADS):
            for a in range(QB // SUBLANES):
                start = BIAS_PERIOD - a * SUBLANES
                first_key = (a * SUBLANES // CHUNK) * CHUNK
                in_band = (col >= first_key) & (col < first_key + (ATT_LEFT_CHUNKS + 1) * CHUNK)
                bias_ref[hd, a * SUBLANES:(a + 1) * SUBLANES, :] = jnp.where(
                    in_band, brow_ref[hd, :, start:start + KW], NEG_INF)
    cur = lax.rem(t, 2)
    nxt = 1 - cur

    @pl.when(t == 0)
    def _start_of_sequence():
        khist[0] = jnp.zeros((HIST, ATT_WIDTH), BF16)
        vhist[0] = jnp.zeros((HIST, ATT_WIDTH), BF16)
        ctail[0] = jnp.zeros((CONV_PAD, CONV_WIDTH), F32)
        st_ref[...] = jnp.zeros_like(st_ref)

    h = h_ref[...]
    xn = _rmsnorm(h, nmix_ref[...]).astype(BF16)
    groups = {}

    def project(g):
        if g not in groups:
            groups[g] = _dot(xn, win_ref[:, g * IN_GROUP:(g + 1) * IN_GROUP])

    def proj(name):
        lo, hi = SEG[name]
        project(lo // IN_GROUP)
        return groups[lo // IN_GROUP][:, lo % IN_GROUP:(hi - 1) % IN_GROUP + 1]

    later_groups = [functools.partial(project, g) for g in range(1, D_IN_PAD // IN_GROUP)]
    o_gla = _gla_mixer(proj, wgate_ref, bgate_ref, gnorm_ref, later_ref, hmask_ref, hsum_ref, st_ref,
                       lambda: later_groups.pop(0)() if later_groups else None)
    o_conv = _conv_mixer(proj, cur, nxt, wdw_ref, bdw_ref, lng_ref, lnb_ref, ctail, cbuf, shifted)
    o_att = _attention_mixer(proj, t, cur, nxt, bias_ref, khist, vhist)
    mixed = jnp.concatenate([o_gla, o_conv, o_att], axis=1).astype(BF16)
    o_ref[...] = h + _dot(mixed, wout_ref[...])


def _gla_mixer(proj, wgate_ref, bgate_ref, gnorm_ref, later_ref, hmask_ref, hsum_ref, st_ref, between_links):
    gq = (proj("gq") * (GLA_DK ** -0.5)).astype(BF16)
    gk = proj("gk")
    gv = proj("gv").astype(BF16)
    z = _dot(proj("lr").astype(BF16), wgate_ref[...]) + bgate_ref[...]
    between_links()
    log_a = (jnp.minimum(z, 0.0) - jnp.log1p(jnp.exp(-jnp.abs(z)))) / GLA_GATE_TAU
    la_hi, la_lo = _split_bf16(log_a)
    later = later_ref[...]
    rev = _dot(later, la_hi) + _dot(later, la_lo)
    between_links()
    kdec = (gk * jnp.exp(rev)).astype(BF16)
    chunk_log = rev + log_a
    hmask = hmask_ref[...]
    chunks = [slice(c * CHUNK, (c + 1) * CHUNK) for c in range(NCH)]
    kv_t = [_dot_tn(gv[rows], kdec[rows]) * hmask for rows in chunks]
    decay = [jnp.exp(chunk_log[rows.start:rows.start + 1, :]) for rows in chunks]
    st = st_ref[...]
    states = []
    for c in range(NCH):
        st = st * decay[c] + kv_t[c]
        states.append(st.astype(BF16))
    st_ref[...] = st
    o_chunks = [_dot_nt(gq[rows], states[c]) for c, rows in enumerate(chunks)]
    og = jnp.concatenate(o_chunks, axis=0)
    sq_hi, sq_lo = _split_bf16(og * og)
    hsum = hsum_ref[...]
    ms = (_dot(sq_hi, hsum) + _dot(sq_lo, hsum)) / GLA_DV
    gg = proj("gg")
    return og * lax.rsqrt(ms + EPS) * gnorm_ref[...] * (gg * jax.nn.sigmoid(gg))


def _conv_mixer(proj, cur, nxt, wdw_ref, bdw_ref, lng_ref, lnb_ref, ctail, cbuf, shifted):
    hg = proj("ca") * jax.nn.sigmoid(proj("cb"))
    cbuf[0:CONV_PAD, :] = ctail[cur]
    cbuf[CONV_PAD:CONV_PAD + TS, :] = hg
    ctail[nxt] = hg[TS - CONV_PAD:TS, :]
    for s in range(1, SUBLANES):
        shifted[s - 1] = cbuf[pl.ds(s, SHIFT_ROWS), :]
    base = CONV_PAD - (CONV_KERNEL - 1)
    rows = []
    for rb in range(TS // CHUNK):
        acc = jnp.zeros((CHUNK, CONV_WIDTH), F32)
        for j in range(CONV_KERNEL):
            a, s = divmod(base + j, SUBLANES)
            src = cbuf if s == 0 else shifted.at[s - 1]
            acc = acc + src[pl.ds(rb * CHUNK + a * SUBLANES, CHUNK), :] * wdw_ref[j:j + 1, :]
        rows.append(acc)
    hc = jnp.concatenate(rows, axis=0) + bdw_ref[...]
    mu = jnp.mean(hc, axis=-1, keepdims=True)
    var = jnp.mean(jnp.square(hc - mu), axis=-1, keepdims=True)
    hc = (hc - mu) * lax.rsqrt(var + EPS) * lng_ref[...] + lnb_ref[...]
    return hc * jax.nn.sigmoid(hc)


def _attention_mixer(proj, t, cur, nxt, bias_ref, khist, vhist):
    aq = proj("aq") * (ATT_HEAD_DIM ** -0.5 * LOG2E)
    kb = proj("ak").astype(BF16)
    vb = proj("av").astype(BF16)
    khist[nxt] = kb
    vhist[nxt] = vb
    lane = lax.broadcasted_iota(jnp.int32, (1, LANES), 1)
    low_half = lane < ATT_HEAD_DIM
    ones_lane = (ATT_HEAD_DIM, 0)
    hist_neg = jnp.where(t == 0, NEG_INF, 0.0)
    one = jnp.ones((), BF16)
    v_ones = {}
    for p in range(ATT_HEADS // 2):
        ls = slice(p * LANES, (p + 1) * LANES)
        for hh, ol in enumerate(ones_lane):
            v_ones[p, hh] = (jnp.where(lane == ol, one, vhist[cur, :, ls]), jnp.where(lane == ol, one, vb[:, ls]))

    blocks = [(p, qb, hh) for p in range(ATT_HEADS // 2) for qb in range(TS // QB) for hh in range(2)]
    live = [{} for _ in blocks]

    def scores(n):
        p, qb, hh = blocks[n]
        ls = slice(p * LANES, (p + 1) * LANES)
        n_hist = HIST - qb * QB
        mine = low_half if hh == 0 else jnp.logical_not(low_half)
        qm = jnp.where(mine, aq[qb * QB:(qb + 1) * QB, ls], 0.0).astype(BF16)
        k_hist = khist[cur, qb * QB:HIST, ls]
        k_cur = kb[0:KW - n_hist, ls]
        live[n]["s"] = jnp.concatenate([_dot_nt(qm, k_hist) + hist_neg, _dot_nt(qm, k_cur)], axis=1)

    def row_max(n):
        p, _, hh = blocks[n]
        s = live[n].pop("s") + bias_ref[2 * p + hh]
        live[n]["s"] = s
        live[n]["m"] = jnp.max(s, axis=-1, keepdims=True)

    def exponent(n):
        live[n]["e"] = jnp.exp2(live[n].pop("s") - live[n].pop("m")).astype(BF16)

    def weighted_values(n):
        p, qb, hh = blocks[n]
        n_hist = HIST - qb * QB
        v_hist, v_cur = v_ones[p, hh][0][qb * QB:HIST], v_ones[p, hh][1][0:KW - n_hist]
        e = live[n].pop("e")
        live[n]["pv"] = _dot(e[:, :n_hist], v_hist) + _dot(e[:, n_hist:], v_cur)

    def normalise(n):
        hh = blocks[n][2]
        pv = live[n].pop("pv")
        live[n]["o"] = pv / pv[:, ones_lane[hh]:ones_lane[hh] + 1]

    stages = (scores, row_max, exponent, weighted_values, normalise)
    for step in range(len(blocks) + len(stages) - 1):
        for depth, stage in enumerate(stages):
            if 0 <= step - depth < len(blocks):
                stage(step - depth)

    o = {blk: live[n]["o"] for n, blk in enumerate(blocks)}
    return jnp.concatenate(
        [jnp.concatenate([jnp.where(low_half, o[p, qb, 0], o[p, qb, 1]) for qb in range(TS // QB)], axis=0)
         for p in range(ATT_HEADS // 2)], axis=1)


def _ffn_kernel(x_hbm, nffn_ref, wup_ref, wdn_ref, nfin_ref, o_hbm, xbuf, obuf, in_sem, out_sem, *, n_tiles,
                final_norm):
    def in_copy(i, slot):
        return pltpu.make_async_copy(x_hbm.at[pl.ds(i * TM, TM), :], xbuf.at[slot], in_sem.at[slot])

    def out_copy(i, slot):
        return pltpu.make_async_copy(obuf.at[slot], o_hbm.at[pl.ds(i * TM, TM), :], out_sem.at[slot])

    ahead = FFN_IN_BUFS - 1
    for j in range(ahead):
        in_copy(j, j).start()

    def tile(i, carry):
        @pl.when(i + ahead < n_tiles)
        def _prefetch():
            in_copy(i + ahead, lax.rem(i + ahead, FFN_IN_BUFS)).start()

        slot, oslot = lax.rem(i, FFN_IN_BUFS), lax.rem(i, FFN_OUT_BUFS)
        in_copy(i, slot).wait()

        @pl.when(i >= FFN_OUT_BUFS)
        def _output_slot_free():
            out_copy(i - FFN_OUT_BUFS, oslot).wait()

        h = xbuf[slot]
        xn = _rmsnorm(h, nffn_ref[...]).astype(BF16)
        acc = h
        for c in range(D_FF // FF_CHUNK):
            u = jnp.maximum(_dot(xn, wup_ref[:, c * FF_CHUNK:(c + 1) * FF_CHUNK]), 0.0)
            acc = acc + _dot((u * u).astype(BF16), wdn_ref[c * FF_CHUNK:(c + 1) * FF_CHUNK, :])
        if final_norm:
            acc = _rmsnorm(acc, nfin_ref[...])
        obuf[oslot] = acc
        out_copy(i, oslot).start()
        return carry

    lax.fori_loop(0, n_tiles, tile, 0)
    for i in range(n_tiles - FFN_OUT_BUFS, n_tiles):
        out_copy(i, i % FFN_OUT_BUFS).wait()


def _const_spec(shape):
    nd = len(shape)
    return pl.BlockSpec(shape, lambda *_: (0,) * nd, pipeline_mode=pl.Buffered(1))


def _mixer_constants():
    i = np.arange(TS)
    later = ((i[None, :] > i[:, None]) & (i[None, :] // CHUNK == i[:, None] // CHUNK))
    v = np.arange(GLA_WIDTH)
    k = np.arange(KEY_PAD)
    hmask = ((v[:, None] // GLA_DV) == (k[None, :] // GLA_DK)) & (k[None, :] < GLA_KEY_WIDTH)
    hsum = (v[:, None] // GLA_DV) == (v[None, :] // GLA_DV)
    return (jnp.asarray(later, BF16), jnp.asarray(hmask, F32), jnp.asarray(hsum, BF16))


def _bias_rows(rel_bias):
    u = np.arange(BIAS_PERIOD)
    dist = np.where(u < KW, HIST - u, HIST + BIAS_PERIOD - u)
    w_idx = np.clip(dist, -MAX_REL_DIST, MAX_REL_DIST) + MAX_REL_DIST
    n_rel = 2 * MAX_REL_DIST + 1
    n_far = HIST - MAX_REL_DIST + 1
    n_mid = min(n_rel - 1, KW - n_far)
    n_near = KW - n_far - n_mid
    pieces = [np.full(n_far, n_rel - 1), np.arange(n_rel - 2, n_rel - 2 - n_mid, -1), np.zeros(n_near, int),
              np.full(BIAS_PERIOD - KW, n_rel - 1)]
    assert KW + QB <= BIAS_PERIOD and (w_idx == np.concatenate(pieces)).all()
    rb = rel_bias.astype(F32) * LOG2E
    lead = rb.shape[:-1]
    far, near = rb[..., n_rel - 1:n_rel], rb[..., 0:1]
    w = jnp.concatenate([jnp.broadcast_to(far, lead + (n_far,)), rb[..., n_rel - 2::-1][..., :n_mid],
                         jnp.broadcast_to(near, lead + (n_near,)),
                         jnp.broadcast_to(far, lead + (BIAS_PERIOD - KW,))], axis=-1)
    advanced = jnp.stack([jnp.roll(w, b, axis=-1) for b in range(SUBLANES)], axis=-2)
    return jnp.concatenate([advanced, advanced], axis=-1)


def _pad_last(w, width):
    return jnp.pad(w, [(0, 0)] * (w.ndim - 1) + [(0, width - w.shape[-1])])


def _pack_w_in(w_in):
    sizes = [s[1] for s in _SEGS]
    splits = np.cumsum(sizes)[:-1].tolist()
    parts = {s[0]: _pad_last(p, s[2]) for p, s in zip(jnp.split(w_in, splits, axis=-1), _SEGS)}
    return jnp.concatenate([parts[name] for name in _PACK_ORDER], axis=-1).astype(BF16)


def _layer_spec(shape, l):
    nd = len(shape) - 1
    return pl.BlockSpec((None,) + tuple(shape[1:]), lambda *_: (l,) + (0,) * nd, pipeline_mode=pl.Buffered(1))


def _mixer_layer(h, l, layered, shared, w_up, w_down):
    B, S, D = h.shape
    nt = S // TS
    steps = B * nt
    up_rows, dn_rows = D_MODEL // steps, D_FF // steps
    assert up_rows % BF16_ROWS == 0 and dn_rows % BF16_ROWS == 0
    in_specs = [pl.BlockSpec((None, TS, D), lambda b, t: (b, t, 0))] + [_layer_spec(a.shape, l) for a in layered] \
        + [_const_spec(a.shape) for a in shared] \
        + [pl.BlockSpec((None, up_rows, D_FF), lambda b, t: (l, b * nt + t, 0)),
           pl.BlockSpec((None, dn_rows, D_MODEL), lambda b, t: (l, b * nt + t, 0))]
    return pl.pallas_call(
        _mixer_kernel,
        grid=(B, nt),
        in_specs=in_specs,
        out_specs=[pl.BlockSpec((None, TS, D), lambda b, t: (b, t, 0)),
                   pl.BlockSpec((up_rows, D_FF), lambda b, t: (b * nt + t, 0)),
                   pl.BlockSpec((dn_rows, D_MODEL), lambda b, t: (b * nt + t, 0))],
        out_shape=[jax.ShapeDtypeStruct((B, S, D), F32), jax.ShapeDtypeStruct((D_MODEL, D_FF), BF16),
                   jax.ShapeDtypeStruct((D_FF, D_MODEL), BF16)],
        scratch_shapes=[pltpu.VMEM((ATT_HEADS, QB, KW), F32),
                        pltpu.VMEM((2, HIST, ATT_WIDTH), BF16), pltpu.VMEM((2, HIST, ATT_WIDTH), BF16),
                        pltpu.VMEM((2, CONV_PAD, CONV_WIDTH), F32),
                        pltpu.VMEM((CONV_PAD + TS, CONV_WIDTH), F32),
                        pltpu.VMEM((SUBLANES - 1, SHIFT_ROWS, CONV_WIDTH), F32),
                        pltpu.VMEM((GLA_WIDTH, KEY_PAD), F32)],
        compiler_params=pltpu.CompilerParams(dimension_semantics=("arbitrary", "arbitrary"),
                                             vmem_limit_bytes=VMEM_LIMIT_BYTES),
        name="mixer",
    )(h, *layered, *shared, w_up, w_down)


def _ffn_layer(h2, l, nffn, wup_bf, wdn_bf, nfin, final_norm):
    T, D = h2.shape
    n_tiles = T // TM
    assert n_tiles >= max(FFN_IN_BUFS, FFN_OUT_BUFS)
    hbm, vmem = pl.BlockSpec(memory_space=pl.ANY), pl.BlockSpec(memory_space=pltpu.VMEM)
    return pl.pallas_call(
        functools.partial(_ffn_kernel, n_tiles=n_tiles, final_norm=final_norm),
        in_specs=[hbm, vmem, vmem, vmem, vmem],
        out_specs=hbm,
        out_shape=jax.ShapeDtypeStruct((T, D), F32),
        scratch_shapes=[pltpu.VMEM((FFN_IN_BUFS, TM, D), F32), pltpu.VMEM((FFN_OUT_BUFS, TM, D), F32),
                        pltpu.SemaphoreType.DMA((FFN_IN_BUFS,)), pltpu.SemaphoreType.DMA((FFN_OUT_BUFS,))],
        compiler_params=pltpu.CompilerParams(vmem_limit_bytes=VMEM_LIMIT_BYTES),
        name="ffn",
    )(h2, nffn[l], wup_bf, wdn_bf, nfin)


def kernel(x, norm_mix, w_in, w_gla_gate, b_gla_gate, gla_norm, w_dw, b_dw, conv_ln_g, conv_ln_b, rel_bias, w_out,
           norm_ffn, w_up, w_down, norm_final):
    B, S, D = x.shape
    assert (D, S % (2 * TS), (B * S) % TM) == (D_MODEL, 0, 0)
    rows = lambda a: a[:, None, :].astype(F32)
    wgate = jnp.pad(w_gla_gate, ((0, 0), (0, LR_PAD - GLA_GATE_RANK), (0, KEY_PAD - GLA_KEY_WIDTH))).astype(BF16)
    mixer_params = (rows(norm_mix), _pack_w_in(w_in), wgate, _pad_last(rows(b_gla_gate), KEY_PAD), rows(gla_norm),
                    w_dw.astype(F32), rows(b_dw), rows(conv_ln_g), rows(conv_ln_b), _bias_rows(rel_bias),
                    w_out.astype(BF16))
    nffn = rows(norm_ffn)
    shared = _mixer_constants()
    nfin = norm_final[None, :].astype(F32)
    h = x
    for l in range(DEPTH):
        h, wup_bf, wdn_bf = _mixer_layer(h, l, mixer_params, shared, w_up, w_down)
        h = _ffn_layer(h.reshape(B * S, D), l, nffn, wup_bf, wdn_bf, nfin,
                       final_norm=(l == DEPTH - 1)).reshape(B, S, D)
    return h
```

```python
import functools

import numpy as np
import jax
import jax.numpy as jnp
from jax import lax
from jax.experimental import pallas as pl
from jax.experimental.pallas import tpu as pltpu

D_MODEL = 1024
DEPTH = 2
CHUNK = 64
GLA_DV, GLA_DK, GLA_HEADS = 96, 48, 4
GLA_WIDTH = GLA_HEADS * GLA_DV
GLA_KEY_WIDTH = GLA_HEADS * GLA_DK
GLA_GATE_RANK = 16
GLA_GATE_TAU = 16.0
CONV_WIDTH = 256
CONV_KERNEL = 31
ATT_HEAD_DIM = 64
ATT_HEADS = 6
ATT_WIDTH = ATT_HEADS * ATT_HEAD_DIM
ATT_LEFT_CHUNKS = 8
MAX_REL_DIST = 128
D_FF = 4 * D_MODEL
EPS = 1e-6
LOG2E = 1.4426950408889634
NEG_INF = -1e30

LANES = 128
SUBLANES = 8
BF16_ROWS = 2 * SUBLANES
VMEM_LIMIT_BYTES = 56 * 1024 * 1024

TS = 512
NCH = TS // CHUNK
HIST = ATT_LEFT_CHUNKS * CHUNK
QB = 128
KW = QB + HIST
BIAS_PERIOD = 1024
CONV_PAD = 32
SHIFT_ROWS = TS + CONV_PAD - SUBLANES
TM = 1024
FF_CHUNK = 1024

KEY_PAD = 256
LR_PAD = LANES
_SEGS = (("gq", GLA_KEY_WIDTH, KEY_PAD), ("gk", GLA_KEY_WIDTH, KEY_PAD), ("gv", GLA_WIDTH, GLA_WIDTH),
         ("gg", GLA_WIDTH, GLA_WIDTH), ("lr", GLA_GATE_RANK, LR_PAD), ("ca", CONV_WIDTH, CONV_WIDTH),
         ("cb", CONV_WIDTH, CONV_WIDTH), ("aq", ATT_WIDTH, ATT_WIDTH), ("ak", ATT_WIDTH, ATT_WIDTH),
         ("av", ATT_WIDTH, ATT_WIDTH))
_PACK_ORDER = ("gq", "gk", "gv", "lr", "gg", "aq", "ca", "cb", "ak", "av")
IN_GROUP = 1024
SEG = {}
_off = 0
for _name in _PACK_ORDER:
    _wp = {s[0]: s[2] for s in _SEGS}[_name]
    assert _off // IN_GROUP == (_off + _wp - 1) // IN_GROUP
    SEG[_name] = (_off, _off + _wp)
    _off += _wp
D_IN_PAD = _off
assert D_IN_PAD % IN_GROUP == 0

F32 = jnp.float32
BF16 = jnp.bfloat16


def _dot(a, b):
    return jnp.dot(a, b, preferred_element_type=F32)


def _dot_nt(a, b):
    return lax.dot_general(a, b, (((1,), (1,)), ((), ())), preferred_element_type=F32)


def _dot_tn(a, b):
    return lax.dot_general(a, b, (((0,), (0,)), ((), ())), preferred_element_type=F32)


def _split_bf16(x):
    hi = x.astype(BF16)
    lo = (x - hi.astype(F32)).astype(BF16)
    return hi, lo


def _rmsnorm(x, g):
    return x * lax.rsqrt(jnp.mean(x * x, axis=-1, keepdims=True) + EPS) * g


def _mixer_kernel(h_ref, nmix_ref, win_ref, wgate_ref, bgate_ref, gnorm_ref, wdw_ref, bdw_ref, lng_ref, lnb_ref,
                  brow_ref, wout_ref, later_ref, hmask_ref, hsum_ref, wup_ref, wdn_ref, o_ref, wup_bf_ref, wdn_bf_ref,
                  bias_ref, khist, vhist, ctail, cbuf, shifted, st_ref):
    t = pl.program_id(1)
    wup_bf_ref[...] = wup_ref[...].astype(BF16)
    wdn_bf_ref[...] = wdn_ref[...].astype(BF16)

    @pl.when((pl.program_id(0) == 0) & (t == 0))
    def _build_bias_table():
        col = lax.broadcasted_iota(jnp.int32, (SUBLANES, KW), 1)
        for hd in range(ATT_HEADS):
            for a in range(QB // SUBLANES):
                start = BIAS_PERIOD - a * SUBLANES
                first_key = (a * SUBLANES // CHUNK) * CHUNK
                in_band = (col >= first_key) & (col < first_key + (ATT_LEFT_CHUNKS + 1) * CHUNK)
                bias_ref[hd, a * SUBLANES:(a + 1) * SUBLANES, :] = jnp.where(
                    in_band, brow_ref[hd, :, start:start + KW], NEG_INF)
    cur = lax.rem(t, 2)
    nxt = 1 - cur

    @pl.when(t == 0)
    def _start_of_sequence():
        khist[0] = jnp.zeros((HIST, ATT_WIDTH), BF16)
        vhist[0] = jnp.zeros((HIST, ATT_WIDTH), BF16)
        ctail[0] = jnp.zeros((CONV_PAD, CONV_WIDTH), F32)
        st_ref[...] = jnp.zeros_like(st_ref)

    h = h_ref[...]
    xn = _rmsnorm(h, nmix_ref[...]).astype(BF16)
    groups = {}

    def project(g):
        if g not in groups:
            groups[g] = _dot(xn, win_ref[:, g * IN_GROUP:(g + 1) * IN_GROUP])

    def proj(name):
        lo, hi = SEG[name]
        project(lo // IN_GROUP)
        return groups[lo // IN_GROUP][:, lo % IN_GROUP:(hi - 1) % IN_GROUP + 1]

    later_groups = [functools.partial(project, g) for g in range(1, D_IN_PAD // IN_GROUP)]
    o_gla = _gla_mixer(proj, wgate_ref, bgate_ref, gnorm_ref, later_ref, hmask_ref, hsum_ref, st_ref,
                       lambda: later_groups.pop(0)() if later_groups else None)
    o_conv = _conv_mixer(proj, cur, nxt, wdw_ref, bdw_ref, lng_ref, lnb_ref, ctail, cbuf, shifted)
    o_att = _attention_mixer(proj, t, cur, nxt, bias_ref, khist, vhist)
    mixed = jnp.concatenate([o_gla, o_conv, o_att], axis=1).astype(BF16)
    o_ref[...] = h + _dot(mixed, wout_ref[...])


def _gla_mixer(proj, wgate_ref, bgate_ref, gnorm_ref, later_ref, hmask_ref, hsum_ref, st_ref, between_links):
    gq = (proj("gq") * (GLA_DK ** -0.5)).astype(BF16)
    gk = proj("gk")
    gv = proj("gv").astype(BF16)
    z = _dot(proj("lr").astype(BF16), wgate_ref[...]) + bgate_ref[...]
    between_links()
    log_a = (jnp.minimum(z, 0.0) - jnp.log1p(jnp.exp(-jnp.abs(z)))) / GLA_GATE_TAU
    la_hi, la_lo = _split_bf16(log_a)
    later = later_ref[...]
    rev = _dot(later, la_hi) + _dot(later, la_lo)
    between_links()
    kdec = (gk * jnp.exp(rev)).astype(BF16)
    chunk_log = rev + log_a
    hmask = hmask_ref[...]
    chunks = [slice(c * CHUNK, (c + 1) * CHUNK) for c in range(NCH)]
    kv_t = [_dot_tn(gv[rows], kdec[rows]) * hmask for rows in chunks]
    decay = [jnp.exp(chunk_log[rows.start:rows.start + 1, :]) for rows in chunks]
    st = st_ref[...]
    states = []
    for c in range(NCH):
        st = st * decay[c] + kv_t[c]
        states.append(st.astype(BF16))
    st_ref[...] = st
    o_chunks = [_dot_nt(gq[rows], states[c]) for c, rows in enumerate(chunks)]
    og = jnp.concatenate(o_chunks, axis=0)
    sq_hi, sq_lo = _split_bf16(og * og)
    hsum = hsum_ref[...]
    ms = (_dot(sq_hi, hsum) + _dot(sq_lo, hsum)) / GLA_DV
    gg = proj("gg")
    return og * lax.rsqrt(ms + EPS) * gnorm_ref[...] * (gg * jax.nn.sigmoid(gg))


def _conv_mixer(proj, cur, nxt, wdw_ref, bdw_ref, lng_ref, lnb_ref, ctail, cbuf, shifted):
    hg = proj("ca") * jax.nn.sigmoid(proj("cb"))
    cbuf[0:CONV_PAD, :] = ctail[cur]
    cbuf[CONV_PAD:CONV_PAD + TS, :] = hg
    ctail[nxt] = hg[TS - CONV_PAD:TS, :]
    for s in range(1, SUBLANES):
        shifted[s - 1] = cbuf[pl.ds(s, SHIFT_ROWS), :]
    base = CONV_PAD - (CONV_KERNEL - 1)
    rows = []
    for rb in range(TS // CHUNK):
        acc = jnp.zeros((CHUNK, CONV_WIDTH), F32)
        for j in range(CONV_KERNEL):
            a, s = divmod(base + j, SUBLANES)
            src = cbuf if s == 0 else shifted.at[s - 1]
            acc = acc + src[pl.ds(rb * CHUNK + a * SUBLANES, CHUNK), :] * wdw_ref[j:j + 1, :]
        rows.append(acc)
    hc = jnp.concatenate(rows, axis=0) + bdw_ref[...]
    mu = jnp.mean(hc, axis=-1, keepdims=True)
    var = jnp.mean(jnp.square(hc - mu), axis=-1, keepdims=True)
    hc = (hc - mu) * lax.rsqrt(var + EPS) * lng_ref[...] + lnb_ref[...]
    return hc * jax.nn.sigmoid(hc)


def _attention_mixer(proj, t, cur, nxt, bias_ref, khist, vhist):
    aq = proj("aq") * (ATT_HEAD_DIM ** -0.5 * LOG2E)
    kb = proj("ak").astype(BF16)
    vb = proj("av").astype(BF16)
    khist[nxt] = kb
    vhist[nxt] = vb
    lane = lax.broadcasted_iota(jnp.int32, (1, LANES), 1)
    low_half = lane < ATT_HEAD_DIM
    ones_lane = (ATT_HEAD_DIM, 0)
    hist_neg = jnp.where(t == 0, NEG_INF, 0.0)
    one = jnp.ones((), BF16)
    v_ones = {}
    for p in range(ATT_HEADS // 2):
        ls = slice(p * LANES, (p + 1) * LANES)
        for hh, ol in enumerate(ones_lane):
            v_ones[p, hh] = (jnp.where(lane == ol, one, vhist[cur, :, ls]), jnp.where(lane == ol, one, vb[:, ls]))

    blocks = [(p, qb, hh) for p in range(ATT_HEADS // 2) for qb in range(TS // QB) for hh in range(2)]
    live = [{} for _ in blocks]

    def scores(n):
        p, qb, hh = blocks[n]
        ls = slice(p * LANES, (p + 1) * LANES)
        n_hist = HIST - qb * QB
        mine = low_half if hh == 0 else jnp.logical_not(low_half)
        qm = jnp.where(mine, aq[qb * QB:(qb + 1) * QB, ls], 0.0).astype(BF16)
        k_hist = khist[cur, qb * QB:HIST, ls]
        k_cur = kb[0:KW - n_hist, ls]
        live[n]["s"] = jnp.concatenate([_dot_nt(qm, k_hist) + hist_neg, _dot_nt(qm, k_cur)], axis=1)

    def row_max(n):
        p, _, hh = blocks[n]
        s = live[n].pop("s") + bias_ref[2 * p + hh]
        live[n]["s"] = s
        live[n]["m"] = jnp.max(s, axis=-1, keepdims=True)

    def exponent(n):
        live[n]["e"] = jnp.exp2(live[n].pop("s") - live[n].pop("m")).astype(BF16)

    def weighted_values(n):
        p, qb, hh = blocks[n]
        n_hist = HIST - qb * QB
        v_hist, v_cur = v_ones[p, hh][0][qb * QB:HIST], v_ones[p, hh][1][0:KW - n_hist]
        e = live[n].pop("e")
        live[n]["pv"] = _dot(e[:, :n_hist], v_hist) + _dot(e[:, n_hist:], v_cur)

    def normalise(n):
        hh = blocks[n][2]
        pv = live[n].pop("pv")
        live[n]["o"] = pv / pv[:, ones_lane[hh]:ones_lane[hh] + 1]

    stages = (scores, row_max, exponent, weighted_values, normalise)
    for step in range(len(blocks) + len(stages) - 1):
        for depth, stage in enumerate(stages):
            if 0 <= step - depth < len(blocks):
                stage(step - depth)

    o = {blk: live[n]["o"] for n, blk in enumerate(blocks)}
    return jnp.concatenate(
        [jnp.concatenate([jnp.where(low_half, o[p, qb, 0], o[p, qb, 1]) for qb in range(TS // QB)], axis=0)
         for p in range(ATT_HEADS // 2)], axis=1)


def _ffn_kernel(h_ref, nffn_ref, wup_ref, wdn_ref, nfin_ref, *rest, final_norm, pack_next):
    if pack_next:
        win_ref, wout_ref, o_ref, win_bf_ref, wout_bf_ref = rest
        src = 0
        for name, width, padded in _SEGS:
            lo = SEG[name][0]
            win_bf_ref[:, lo:lo + width] = win_ref[:, src:src + width].astype(BF16)
            if padded > width:
                win_bf_ref[:, lo + width:lo + padded] = jnp.zeros((win_bf_ref.shape[0], padded - width), BF16)
            src += width
        wout_bf_ref[...] = wout_ref[...].astype(BF16)
    else:
        o_ref, = rest
    h = h_ref[...]
    xn = _rmsnorm(h, nffn_ref[...]).astype(BF16)
    acc = h
    for c in range(D_FF // FF_CHUNK):
        u = jnp.maximum(_dot(xn, wup_ref[:, c * FF_CHUNK:(c + 1) * FF_CHUNK]), 0.0)
        acc = acc + _dot((u * u).astype(BF16), wdn_ref[c * FF_CHUNK:(c + 1) * FF_CHUNK, :])
    if final_norm:
        acc = _rmsnorm(acc, nfin_ref[...])
    o_ref[...] = acc


def _const_spec(shape):
    nd = len(shape)
    return pl.BlockSpec(shape, lambda *_: (0,) * nd, pipeline_mode=pl.Buffered(1))


def _mixer_constants():
    i = np.arange(TS)
    later = ((i[None, :] > i[:, None]) & (i[None, :] // CHUNK == i[:, None] // CHUNK))
    v = np.arange(GLA_WIDTH)
    k = np.arange(KEY_PAD)
    hmask = ((v[:, None] // GLA_DV) == (k[None, :] // GLA_DK)) & (k[None, :] < GLA_KEY_WIDTH)
    hsum = (v[:, None] // GLA_DV) == (v[None, :] // GLA_DV)
    return (jnp.asarray(later, BF16), jnp.asarray(hmask, F32), jnp.asarray(hsum, BF16))


def _bias_rows(rel_bias):
    u = np.arange(BIAS_PERIOD)
    dist = np.where(u < KW, HIST - u, HIST + BIAS_PERIOD - u)
    w_idx = np.clip(dist, -MAX_REL_DIST, MAX_REL_DIST) + MAX_REL_DIST
    n_rel = 2 * MAX_REL_DIST + 1
    n_far = HIST - MAX_REL_DIST + 1
    n_mid = min(n_rel - 1, KW - n_far)
    n_near = KW - n_far - n_mid
    pieces = [np.full(n_far, n_rel - 1), np.arange(n_rel - 2, n_rel - 2 - n_mid, -1), np.zeros(n_near, int),
              np.full(BIAS_PERIOD - KW, n_rel - 1)]
    assert KW + QB <= BIAS_PERIOD and (w_idx == np.concatenate(pieces)).all()
    rb = rel_bias.astype(F32) * LOG2E
    lead = rb.shape[:-1]
    far, near = rb[..., n_rel - 1:n_rel], rb[..., 0:1]
    w = jnp.concatenate([jnp.broadcast_to(far, lead + (n_far,)), rb[..., n_rel - 2::-1][..., :n_mid],
                         jnp.broadcast_to(near, lead + (n_near,)),
                         jnp.broadcast_to(far, lead + (BIAS_PERIOD - KW,))], axis=-1)
    advanced = jnp.stack([jnp.roll(w, b, axis=-1) for b in range(SUBLANES)], axis=-2)
    return jnp.concatenate([advanced, advanced], axis=-1)


def _pad_last(w, width):
    return jnp.pad(w, [(0, 0)] * (w.ndim - 1) + [(0, width - w.shape[-1])])


def _pack_w_in(w_in):
    sizes = [s[1] for s in _SEGS]
    splits = np.cumsum(sizes)[:-1].tolist()
    parts = {s[0]: _pad_last(p, s[2]) for p, s in zip(jnp.split(w_in, splits, axis=-1), _SEGS)}
    return jnp.concatenate([parts[name] for name in _PACK_ORDER], axis=-1).astype(BF16)


def _layer_spec(shape, l):
    nd = len(shape) - 1
    return pl.BlockSpec((None,) + tuple(shape[1:]), lambda *_: (l,) + (0,) * nd, pipeline_mode=pl.Buffered(1))


def _mixer_layer(h, l, layered, shared, w_up, w_down):
    B, S, D = h.shape
    nt = S // TS
    steps = B * nt
    up_rows, dn_rows = D_MODEL // steps, D_FF // steps
    assert up_rows % BF16_ROWS == 0 and dn_rows % BF16_ROWS == 0
    in_specs = [pl.BlockSpec((None, TS, D), lambda b, t: (b, t, 0))] \
        + [_layer_spec(a.shape, l if a.shape[0] > 1 else 0) for a in layered] \
        + [_const_spec(a.shape) for a in shared] \
        + [pl.BlockSpec((None, up_rows, D_FF), lambda b, t: (l, b * nt + t, 0)),
           pl.BlockSpec((None, dn_rows, D_MODEL), lambda b, t: (l, b * nt + t, 0))]
    return pl.pallas_call(
        _mixer_kernel,
        grid=(B, nt),
        in_specs=in_specs,
        out_specs=[pl.BlockSpec((None, TS, D), lambda b, t: (b, t, 0)),
                   pl.BlockSpec((up_rows, D_FF), lambda b, t: (b * nt + t, 0)),
                   pl.BlockSpec((dn_rows, D_MODEL), lambda b, t: (b * nt + t, 0))],
        out_shape=[jax.ShapeDtypeStruct((B, S, D), F32), jax.ShapeDtypeStruct((D_MODEL, D_FF), BF16),
                   jax.ShapeDtypeStruct((D_FF, D_MODEL), BF16)],
        scratch_shapes=[pltpu.VMEM((ATT_HEADS, QB, KW), F32),
                        pltpu.VMEM((2, HIST, ATT_WIDTH), BF16), pltpu.VMEM((2, HIST, ATT_WIDTH), BF16),
                        pltpu.VMEM((2, CONV_PAD, CONV_WIDTH), F32),
                        pltpu.VMEM((CONV_PAD + TS, CONV_WIDTH), F32),
                        pltpu.VMEM((SUBLANES - 1, SHIFT_ROWS, CONV_WIDTH), F32),
                        pltpu.VMEM((GLA_WIDTH, KEY_PAD), F32)],
        compiler_params=pltpu.CompilerParams(dimension_semantics=("arbitrary", "arbitrary"),
                                             vmem_limit_bytes=VMEM_LIMIT_BYTES),
        name="mixer",
    )(h, *layered, *shared, w_up, w_down)


def _ffn_layer(h2, l, nffn, wup_bf, wdn_bf, nfin, final_norm, next_weights=None):
    T, D = h2.shape
    steps = T // TM
    in_specs = [pl.BlockSpec((TM, D), lambda i: (i, 0)), _layer_spec(nffn.shape, l), _const_spec(wup_bf.shape),
                _const_spec(wdn_bf.shape), _const_spec(nfin.shape)]
    out_specs = [pl.BlockSpec((TM, D), lambda i: (i, 0))]
    out_shape = [jax.ShapeDtypeStruct((T, D), F32)]
    args = [h2, nffn, wup_bf, wdn_bf, nfin]
    if next_weights is not None:
        w_in, w_out = next_weights
        in_rows, out_rows = w_in.shape[1] // steps, w_out.shape[1] // steps
        assert in_rows % BF16_ROWS == 0 and out_rows % BF16_ROWS == 0
        in_specs += [pl.BlockSpec((None, in_rows, w_in.shape[2]), lambda i: (l + 1, i, 0)),
                     pl.BlockSpec((None, out_rows, w_out.shape[2]), lambda i: (l + 1, i, 0))]
        out_specs += [pl.BlockSpec((None, in_rows, D_IN_PAD), lambda i: (0, i, 0)),
                      pl.BlockSpec((None, out_rows, w_out.shape[2]), lambda i: (0, i, 0))]
        out_shape += [jax.ShapeDtypeStruct((1, w_in.shape[1], D_IN_PAD), BF16),
                      jax.ShapeDtypeStruct((1,) + w_out.shape[1:], BF16)]
        args += [w_in, w_out]
    return pl.pallas_call(
        functools.partial(_ffn_kernel, final_norm=final_norm, pack_next=next_weights is not None),
        grid=(steps,),
        in_specs=in_specs,
        out_specs=out_specs,
        out_shape=out_shape,
        compiler_params=pltpu.CompilerParams(dimension_semantics=("arbitrary",),
                                             vmem_limit_bytes=VMEM_LIMIT_BYTES),
        name="ffn",
    )(*args)


def kernel(x, norm_mix, w_in, w_gla_gate, b_gla_gate, gla_norm, w_dw, b_dw, conv_ln_g, conv_ln_b, rel_bias, w_out,
           norm_ffn, w_up, w_down, norm_final):
    B, S, D = x.shape
    assert (D, S % (2 * TS), (B * S) % TM) == (D_MODEL, 0, 0)
    rows = lambda a: a[:, None, :].astype(F32)
    wgate = jnp.pad(w_gla_gate, ((0, 0), (0, LR_PAD - GLA_GATE_RANK), (0, KEY_PAD - GLA_KEY_WIDTH))).astype(BF16)
    win_bf, wout_bf = _pack_w_in(w_in[:1]), w_out[:1].astype(BF16)
    nffn = rows(norm_ffn)
    shared = _mixer_constants()
    nfin = norm_final[None, :].astype(F32)
    bias_rows = _bias_rows(rel_bias)
    h = x
    for l in range(DEPTH):
        mixer_params = (rows(norm_mix), win_bf, wgate, _pad_last(rows(b_gla_gate), KEY_PAD), rows(gla_norm),
                        w_dw.astype(F32), rows(b_dw), rows(conv_ln_g), rows(conv_ln_b), bias_rows, wout_bf)
        h, wup_bf, wdn_bf = _mixer_layer(h, l, mixer_params, shared, w_up, w_down)
        last = l == DEPTH - 1
        outs = _ffn_layer(h.reshape(B * S, D), l, nffn, wup_bf, wdn_bf, nfin, final_norm=last,
                          next_weights=None if last else (w_in.astype(F32), w_out.astype(F32)))
        if not last:
            win_bf, wout_bf = outs[1], outs[2]
        h = outs[0].reshape(B, S, D)
    return h
```

```python
import functools

import numpy as np
import jax
import jax.numpy as jnp
from jax import lax
from jax.experimental import pallas as pl
from jax.experimental.pallas import tpu as pltpu

D_MODEL = 1024
DEPTH = 2
CHUNK = 64
GLA_DV, GLA_DK, GLA_HEADS = 96, 48, 4
GLA_WIDTH = GLA_HEADS * GLA_DV
GLA_KEY_WIDTH = GLA_HEADS * GLA_DK
GLA_GATE_RANK = 16
GLA_GATE_TAU = 16.0
CONV_WIDTH = 256
CONV_KERNEL = 31
ATT_HEAD_DIM = 64
ATT_HEADS = 6
ATT_WIDTH = ATT_HEADS * ATT_HEAD_DIM
ATT_LEFT_CHUNKS = 8
MAX_REL_DIST = 128
D_FF = 4 * D_MODEL
EPS = 1e-6
LOG2E = 1.4426950408889634
NEG_INF = -1e30

LANES = 128
SUBLANES = 8
BF16_ROWS = 2 * SUBLANES
VMEM_LIMIT_BYTES = 56 * 1024 * 1024

TS = 512
NCH = TS // CHUNK
HIST = ATT_LEFT_CHUNKS * CHUNK
QB = 128
KW = QB + HIST
BIAS_PERIOD = 1024
CONV_PAD = 32
SHIFT_ROWS = TS + CONV_PAD - SUBLANES
TM = 1024
FF_CHUNK = 1024

KEY_PAD = 256
LR_PAD = LANES
_SEGS = (("gq", GLA_KEY_WIDTH, KEY_PAD), ("gk", GLA_KEY_WIDTH, KEY_PAD), ("gv", GLA_WIDTH, GLA_WIDTH),
         ("gg", GLA_WIDTH, GLA_WIDTH), ("lr", GLA_GATE_RANK, LR_PAD), ("ca", CONV_WIDTH, CONV_WIDTH),
         ("cb", CONV_WIDTH, CONV_WIDTH), ("aq", ATT_WIDTH, ATT_WIDTH), ("ak", ATT_WIDTH, ATT_WIDTH),
         ("av", ATT_WIDTH, ATT_WIDTH))
_PACK_ORDER = ("gq", "gk", "gv", "lr", "gg", "aq", "ca", "cb", "ak", "av")
IN_GROUP = 1024
SEG = {}
_off = 0
for _name in _PACK_ORDER:
    _wp = {s[0]: s[2] for s in _SEGS}[_name]
    assert _off // IN_GROUP == (_off + _wp - 1) // IN_GROUP
    SEG[_name] = (_off, _off + _wp)
    _off += _wp
D_IN_PAD = _off
assert D_IN_PAD % IN_GROUP == 0

F32 = jnp.float32
BF16 = jnp.bfloat16


def _dot(a, b):
    return jnp.dot(a, b, preferred_element_type=F32)


def _dot_nt(a, b):
    return lax.dot_general(a, b, (((1,), (1,)), ((), ())), preferred_element_type=F32)


def _dot_tn(a, b):
    return lax.dot_general(a, b, (((0,), (0,)), ((), ())), preferred_element_type=F32)


def _split_bf16(x):
    hi = x.astype(BF16)
    lo = (x - hi.astype(F32)).astype(BF16)
    return hi, lo


def _rmsnorm(x, g):
    return x * lax.rsqrt(jnp.mean(x * x, axis=-1, keepdims=True) + EPS) * g


def _mixer_kernel(h_ref, nmix_ref, win_ref, wgate_ref, bgate_ref, gnorm_ref, wdw_ref, bdw_ref, lng_ref, lnb_ref,
                  brow_ref, wout_ref, later_ref, hmask_ref, hsum_ref, wup_ref, wdn_ref, o_ref, wup_bf_ref, wdn_bf_ref,
                  bias_ref, khist, vhist, ctail, cbuf, shifted, st_ref):
    t = pl.program_id(1)
    wup_bf_ref[...] = wup_ref[...].astype(BF16)
    wdn_bf_ref[...] = wdn_ref[...].astype(BF16)

    @pl.when((pl.program_id(0) == 0) & (t == 0))
    def _build_bias_table():
        col = lax.broadcasted_iota(jnp.int32, (SUBLANES, KW), 1)
        for hd in range(ATT_HEADS):
            for a in range(QB // SUBLANES):
                start = BIAS_PERIOD - a * SUBLANES
                first_key = (a * SUBLANES // CHUNK) * CHUNK
                in_band = (col >= first_key) & (col < first_key + (ATT_LEFT_CHUNKS + 1) * CHUNK)
                bias_ref[hd, a * SUBLANES:(a + 1) * SUBLANES, :] = jnp.where(
                    in_band, brow_ref[hd, :, start:start + KW], NEG_INF)
    cur = lax.rem(t, 2)
    nxt = 1 - cur

    @pl.when(t == 0)
    def _start_of_sequence():
        khist[0] = jnp.zeros((HIST, ATT_WIDTH), BF16)
        vhist[0] = jnp.zeros((HIST, ATT_WIDTH), BF16)
        ctail[0] = jnp.zeros((CONV_PAD, CONV_WIDTH), F32)
        st_ref[...] = jnp.zeros_like(st_ref)

    h = h_ref[...]
    xn = _rmsnorm(h, nmix_ref[...]).astype(BF16)
    groups = {}

    def project(g):
        if g not in groups:
            groups[g] = _dot(xn, win_ref[:, g * IN_GROUP:(g + 1) * IN_GROUP])

    def proj(name):
        lo, hi = SEG[name]
        project(lo // IN_GROUP)
        return groups[lo // IN_GROUP][:, lo % IN_GROUP:(hi - 1) % IN_GROUP + 1]

    later_groups = [functools.partial(project, g) for g in range(1, D_IN_PAD // IN_GROUP)]
    o_gla = _gla_mixer(proj, wgate_ref, bgate_ref, gnorm_ref, later_ref, hmask_ref, hsum_ref, st_ref,
                       lambda: later_groups.pop(0)() if later_groups else None)
    o_conv = _conv_mixer(proj, cur, nxt, wdw_ref, bdw_ref, lng_ref, lnb_ref, ctail, cbuf, shifted)
    o_att = _attention_mixer(proj, t, cur, nxt, bias_ref, khist, vhist)
    mixed = jnp.concatenate([o_gla, o_conv, o_att], axis=1).astype(BF16)
    o_ref[...] = h + _dot(mixed, wout_ref[...])


def _gla_mixer(proj, wgate_ref, bgate_ref, gnorm_ref, later_ref, hmask_ref, hsum_ref, st_ref, between_links):
    gq = (proj("gq") * (GLA_DK ** -0.5)).astype(BF16)
    gk = proj("gk")
    gv = proj("gv").astype(BF16)
    z = _dot(proj("lr").astype(BF16), wgate_ref[...]) + bgate_ref[...]
    between_links()
    log_a = (jnp.minimum(z, 0.0) - jnp.log1p(jnp.exp(-jnp.abs(z)))) / GLA_GATE_TAU
    la_hi, la_lo = _split_bf16(log_a)
    later = later_ref[...]
    rev = _dot(later, la_hi) + _dot(later, la_lo)
    between_links()
    kdec = (gk * jnp.exp(rev)).astype(BF16)
    chunk_log = rev + log_a
    hmask = hmask_ref[...]
    chunks = [slice(c * CHUNK, (c + 1) * CHUNK) for c in range(NCH)]
    kv_t = [_dot_tn(gv[rows], kdec[rows]) * hmask for rows in chunks]
    decay = [jnp.exp(chunk_log[rows.start:rows.start + 1, :]) for rows in chunks]
    st = st_ref[...]
    states = []
    for c in range(NCH):
        st = st * decay[c] + kv_t[c]
        states.append(st.astype(BF16))
    st_ref[...] = st
    o_chunks = [_dot_nt(gq[rows], states[c]) for c, rows in enumerate(chunks)]
    og = jnp.concatenate(o_chunks, axis=0)
    sq_hi, sq_lo = _split_bf16(og * og)
    hsum = hsum_ref[...]
    ms = (_dot(sq_hi, hsum) + _dot(sq_lo, hsum)) / GLA_DV
    gg = proj("gg")
    return og * lax.rsqrt(ms + EPS) * gnorm_ref[...] * (gg * jax.nn.sigmoid(gg))


def _conv_mixer(proj, cur, nxt, wdw_ref, bdw_ref, lng_ref, lnb_ref, ctail, cbuf, shifted):
    hg = proj("ca") * jax.nn.sigmoid(proj("cb"))
    cbuf[0:CONV_PAD, :] = ctail[cur]
    cbuf[CONV_PAD:CONV_PAD + TS, :] = hg
    ctail[nxt] = hg[TS - CONV_PAD:TS, :]
    for s in range(1, SUBLANES):
        shifted[s - 1] = cbuf[pl.ds(s, SHIFT_ROWS), :]
    base = CONV_PAD - (CONV_KERNEL - 1)
    rows = []
    for rb in range(TS // CHUNK):
        acc = jnp.zeros((CHUNK, CONV_WIDTH), F32)
        for j in range(CONV_KERNEL):
            a, s = divmod(base + j, SUBLANES)
            src = cbuf if s == 0 else shifted.at[s - 1]
            acc = acc + src[pl.ds(rb * CHUNK + a * SUBLANES, CHUNK), :] * wdw_ref[j:j + 1, :]
        rows.append(acc)
    hc = jnp.concatenate(rows, axis=0) + bdw_ref[...]
    mu = jnp.mean(hc, axis=-1, keepdims=True)
    var = jnp.mean(jnp.square(hc - mu), axis=-1, keepdims=True)
    hc = (hc - mu) * lax.rsqrt(var + EPS) * lng_ref[...] + lnb_ref[...]
    return hc * jax.nn.sigmoid(hc)


def _attention_mixer(proj, t, cur, nxt, bias_ref, khist, vhist):
    aq = proj("aq") * (ATT_HEAD_DIM ** -0.5 * LOG2E)
    kb = proj("ak").astype(BF16)
    vb = proj("av").astype(BF16)
    khist[nxt] = kb
    vhist[nxt] = vb
    lane = lax.broadcasted_iota(jnp.int32, (1, LANES), 1)
    low_half = lane < ATT_HEAD_DIM
    ones_lane = (ATT_HEAD_DIM, 0)
    hist_neg = jnp.where(t == 0, NEG_INF, 0.0)
    one = jnp.ones((), BF16)
    v_ones = {}
    for p in range(ATT_HEADS // 2):
        ls = slice(p * LANES, (p + 1) * LANES)
        for hh, ol in enumerate(ones_lane):
            v_ones[p, hh] = (jnp.where(lane == ol, one, vhist[cur, :, ls]), jnp.where(lane == ol, one, vb[:, ls]))

    blocks = [(p, qb, hh) for qb in range(TS // QB) for p in range(ATT_HEADS // 2) for hh in range(2)]
    live = [{} for _ in blocks]

    def scores(n):
        p, qb, hh = blocks[n]
        ls = slice(p * LANES, (p + 1) * LANES)
        n_hist = HIST - qb * QB
        mine = low_half if hh == 0 else jnp.logical_not(low_half)
        qm = jnp.where(mine, aq[qb * QB:(qb + 1) * QB, ls], 0.0).astype(BF16)
        k_hist = khist[cur, qb * QB:HIST, ls]
        k_cur = kb[0:KW - n_hist, ls]
        live[n]["s"] = jnp.concatenate([_dot_nt(qm, k_hist) + hist_neg, _dot_nt(qm, k_cur)], axis=1)

    def row_max(n):
        p, _, hh = blocks[n]
        s = live[n].pop("s") + bias_ref[2 * p + hh]
        live[n]["s"] = s
        live[n]["m"] = jnp.max(s, axis=-1, keepdims=True)

    def exponent(n):
        live[n]["e"] = jnp.exp2(live[n].pop("s") - live[n].pop("m")).astype(BF16)

    def weighted_values(n):
        p, qb, hh = blocks[n]
        n_hist = HIST - qb * QB
        v_hist, v_cur = v_ones[p, hh][0][qb * QB:HIST], v_ones[p, hh][1][0:KW - n_hist]
        e = live[n].pop("e")
        live[n]["pv"] = _dot(e[:, :n_hist], v_hist) + _dot(e[:, n_hist:], v_cur)

    def normalise(n):
        hh = blocks[n][2]
        pv = live[n].pop("pv")
        live[n]["o"] = pv / pv[:, ones_lane[hh]:ones_lane[hh] + 1]

    stages = (scores, row_max, exponent, weighted_values, normalise)
    for step in range(len(blocks) + len(stages) - 1):
        for depth, stage in enumerate(stages):
            if 0 <= step - depth < len(blocks):
                stage(step - depth)

    o = {blk: live[n]["o"] for n, blk in enumerate(blocks)}
    return jnp.concatenate(
        [jnp.concatenate([jnp.where(low_half, o[p, qb, 0], o[p, qb, 1]) for qb in range(TS // QB)], axis=0)
         for p in range(ATT_HEADS // 2)], axis=1)


def _ffn_kernel(h_ref, nffn_ref, wup_ref, wdn_ref, nfin_ref, *rest, final_norm, pack_next):
    if pack_next:
        win_ref, wout_ref, o_ref, win_bf_ref, wout_bf_ref = rest
        src = 0
        for name, width, padded in _SEGS:
            lo = SEG[name][0]
            win_bf_ref[:, lo:lo + width] = win_ref[:, src:src + width].astype(BF16)
            if padded > width:
                win_bf_ref[:, lo + width:lo + padded] = jnp.zeros((win_bf_ref.shape[0], padded - width), BF16)
            src += width
        wout_bf_ref[...] = wout_ref[...].astype(BF16)
    else:
        o_ref, = rest
    h = h_ref[...]
    xn = _rmsnorm(h, nffn_ref[...]).astype(BF16)
    acc = h
    for c in range(D_FF // FF_CHUNK):
        u = jnp.maximum(_dot(xn, wup_ref[:, c * FF_CHUNK:(c + 1) * FF_CHUNK]), 0.0)
        acc = acc + _dot((u * u).astype(BF16), wdn_ref[c * FF_CHUNK:(c + 1) * FF_CHUNK, :])
    if final_norm:
        acc = _rmsnorm(acc, nfin_ref[...])
    o_ref[...] = acc


def _const_spec(shape):
    nd = len(shape)
    return pl.BlockSpec(shape, lambda *_: (0,) * nd, pipeline_mode=pl.Buffered(1))


def _mixer_constants():
    i = np.arange(TS)
    later = ((i[None, :] > i[:, None]) & (i[None, :] // CHUNK == i[:, None] // CHUNK))
    v = np.arange(GLA_WIDTH)
    k = np.arange(KEY_PAD)
    hmask = ((v[:, None] // GLA_DV) == (k[None, :] // GLA_DK)) & (k[None, :] < GLA_KEY_WIDTH)
    hsum = (v[:, None] // GLA_DV) == (v[None, :] // GLA_DV)
    return (jnp.asarray(later, BF16), jnp.asarray(hmask, F32), jnp.asarray(hsum, BF16))


def _bias_rows(rel_bias):
    u = np.arange(BIAS_PERIOD)
    dist = np.where(u < KW, HIST - u, HIST + BIAS_PERIOD - u)
    w_idx = np.clip(dist, -MAX_REL_DIST, MAX_REL_DIST) + MAX_REL_DIST
    n_rel = 2 * MAX_REL_DIST + 1
    n_far = HIST - MAX_REL_DIST + 1
    n_mid = min(n_rel - 1, KW - n_far)
    n_near = KW - n_far - n_mid
    pieces = [np.full(n_far, n_rel - 1), np.arange(n_rel - 2, n_rel - 2 - n_mid, -1), np.zeros(n_near, int),
              np.full(BIAS_PERIOD - KW, n_rel - 1)]
    assert KW + QB <= BIAS_PERIOD and (w_idx == np.concatenate(pieces)).all()
    rb = rel_bias.astype(F32) * LOG2E
    lead = rb.shape[:-1]
    far, near = rb[..., n_rel - 1:n_rel], rb[..., 0:1]
    w = jnp.concatenate([jnp.broadcast_to(far, lead + (n_far,)), rb[..., n_rel - 2::-1][..., :n_mid],
                         jnp.broadcast_to(near, lead + (n_near,)),
                         jnp.broadcast_to(far, lead + (BIAS_PERIOD - KW,))], axis=-1)
    advanced = jnp.stack([jnp.roll(w, b, axis=-1) for b in range(SUBLANES)], axis=-2)
    return jnp.concatenate([advanced, advanced], axis=-1)


def _pad_last(w, width):
    return jnp.pad(w, [(0, 0)] * (w.ndim - 1) + [(0, width - w.shape[-1])])


def _pack_w_in(w_in):
    sizes = [s[1] for s in _SEGS]
    splits = np.cumsum(sizes)[:-1].tolist()
    parts = {s[0]: _pad_last(p, s[2]) for p, s in zip(jnp.split(w_in, splits, axis=-1), _SEGS)}
    return jnp.concatenate([parts[name] for name in _PACK_ORDER], axis=-1).astype(BF16)


def _layer_spec(shape, l):
    nd = len(shape) - 1
    return pl.BlockSpec((None,) + tuple(shape[1:]), lambda *_: (l,) + (0,) * nd, pipeline_mode=pl.Buffered(1))


def _mixer_layer(h, l, layered, shared, w_up, w_down):
    B, S, D = h.shape
    nt = S // TS
    steps = B * nt
    up_rows, dn_rows = D_MODEL // steps, D_FF // steps
    assert up_rows % BF16_ROWS == 0 and dn_rows % BF16_ROWS == 0
    in_specs = [pl.BlockSpec((None, TS, D), lambda b, t: (b, t, 0))] \
        + [_layer_spec(a.shape, l if a.shape[0] > 1 else 0) for a in layered] \
        + [_const_spec(a.shape) for a in shared] \
        + [pl.BlockSpec((None, up_rows, D_FF), lambda b, t: (l, b * nt + t, 0)),
           pl.BlockSpec((None, dn_rows, D_MODEL), lambda b, t: (l, b * nt + t, 0))]
    return pl.pallas_call(
        _mixer_kernel,
        grid=(B, nt),
        in_specs=in_specs,
        out_specs=[pl.BlockSpec((None, TS, D), lambda b, t: (b, t, 0)),
                   pl.BlockSpec((up_rows, D_FF), lambda b, t: (b * nt + t, 0)),
                   pl.BlockSpec((dn_rows, D_MODEL), lambda b, t: (b * nt + t, 0))],
        out_shape=[jax.ShapeDtypeStruct((B, S, D), F32), jax.ShapeDtypeStruct((D_MODEL, D_FF), BF16),
                   jax.ShapeDtypeStruct((D_FF, D_MODEL), BF16)],
        scratch_shapes=[pltpu.VMEM((ATT_HEADS, QB, KW), F32),
                        pltpu.VMEM((2, HIST, ATT_WIDTH), BF16), pltpu.VMEM((2, HIST, ATT_WIDTH), BF16),
                        pltpu.VMEM((2, CONV_PAD, CONV_WIDTH), F32),
                        pltpu.VMEM((CONV_PAD + TS, CONV_WIDTH), F32),
                        pltpu.VMEM((SUBLANES - 1, SHIFT_ROWS, CONV_WIDTH), F32),
                        pltpu.VMEM((GLA_WIDTH, KEY_PAD), F32)],
        compiler_params=pltpu.CompilerParams(dimension_semantics=("arbitrary", "arbitrary"),
                                             vmem_limit_bytes=VMEM_LIMIT_BYTES),
        name="mixer",
    )(h, *layered, *shared, w_up, w_down)


def _ffn_layer(h2, l, nffn, wup_bf, wdn_bf, nfin, final_norm, next_weights=None):
    T, D = h2.shape
    steps = T // TM
    in_specs = [pl.BlockSpec((TM, D), lambda i: (i, 0)), _layer_spec(nffn.shape, l), _const_spec(wup_bf.shape),
                _const_spec(wdn_bf.shape), _const_spec(nfin.shape)]
    out_specs = [pl.BlockSpec((TM, D), lambda i: (i, 0))]
    out_shape = [jax.ShapeDtypeStruct((T, D), F32)]
    args = [h2, nffn, wup_bf, wdn_bf, nfin]
    if next_weights is not None:
        w_in, w_out = next_weights
        in_rows, out_rows = w_in.shape[1] // steps, w_out.shape[1] // steps
        assert in_rows % BF16_ROWS == 0 and out_rows % BF16_ROWS == 0
        in_specs += [pl.BlockSpec((None, in_rows, w_in.shape[2]), lambda i: (l + 1, i, 0)),
                     pl.BlockSpec((None, out_rows, w_out.shape[2]), lambda i: (l + 1, i, 0))]
        out_specs += [pl.BlockSpec((None, in_rows, D_IN_PAD), lambda i: (0, i, 0)),
                      pl.BlockSpec((None, out_rows, w_out.shape[2]), lambda i: (0, i, 0))]
        out_shape += [jax.ShapeDtypeStruct((1, w_in.shape[1], D_IN_PAD), BF16),
                      jax.ShapeDtypeStruct((1,) + w_out.shape[1:], BF16)]
        args += [w_in, w_out]
    return pl.pallas_call(
        functools.partial(_ffn_kernel, final_norm=final_norm, pack_next=next_weights is not None),
        grid=(steps,),
        in_specs=in_specs,
        out_specs=out_specs,
        out_shape=out_shape,
        compiler_params=pltpu.CompilerParams(dimension_semantics=("arbitrary",),
                                             vmem_limit_bytes=VMEM_LIMIT_BYTES),
        name="ffn",
    )(*args)


def kernel(x, norm_mix, w_in, w_gla_gate, b_gla_gate, gla_norm, w_dw, b_dw, conv_ln_g, conv_ln_b, rel_bias, w_out,
           norm_ffn, w_up, w_down, norm_final):
    B, S, D = x.shape
    assert (D, S % (2 * TS), (B * S) % TM) == (D_MODEL, 0, 0)
    rows = lambda a: a[:, None, :].astype(F32)
    wgate = jnp.pad(w_gla_gate, ((0, 0), (0, LR_PAD - GLA_GATE_RANK), (0, KEY_PAD - GLA_KEY_WIDTH))).astype(BF16)
    win_bf, wout_bf = _pack_w_in(w_in[:1]), w_out[:1].astype(BF16)
    nffn = rows(norm_ffn)
    shared = _mixer_constants()
    nfin = norm_final[None, :].astype(F32)
    bias_rows = _bias_rows(rel_bias)
    h = x
    for l in range(DEPTH):
        mixer_params = (rows(norm_mix), win_bf, wgate, _pad_last(rows(b_gla_gate), KEY_PAD), rows(gla_norm),
                        w_dw.astype(F32), rows(b_dw), rows(conv_ln_g), rows(conv_ln_b), bias_rows, wout_bf)
        h, wup_bf, wdn_bf = _mixer_layer(h, l, mixer_params, shared, w_up, w_down)
        last = l == DEPTH - 1
        outs = _ffn_layer(h.reshape(B * S, D), l, nffn, wup_bf, wdn_bf, nfin, final_norm=last,
                          next_weights=None if last else (w_in.astype(F32), w_out.astype(F32)))
        if not last:
            win_bf, wout_bf = outs[1], outs[2]
        h = outs[0].reshape(B, S, D)
    return h
```
